```python
import math
import jax, jax.numpy as jnp
from jax import lax
import numpy as np

D_MODEL = 4096
BATCH = 2
SEQ = 8192
DEPTH = 4

GRID_W = 64
CTX_LEN = 256
EPS = 1e-6

RWKV_HEADS = 16
RWKV_HD = 64
RWKV_W = RWKV_HEADS * RWKV_HD
RWKV_LORA = 64
RWKV_SHIFT = 3
RWKV_CONV_CH = 3 * RWKV_W + 4 * RWKV_LORA
RWKV_GN_EPS = 64e-5

DIFF_HEADS = 8
DIFF_HD = 64
DIFF_VD = 2 * DIFF_HD
DIFF_W = DIFF_HEADS * DIFF_VD
Q_BLOCK = 128
ROPE_BASE = 10000.0

SSD_HEADS = 32
SSD_HD = 64
SSD_W = SSD_HEADS * SSD_HD
SSD_GROUPS = 8
SSD_STATE = 128
SSD_CONV = 5
SSD_CHUNK = 128
SSD_XBC = SSD_W + 2 * SSD_GROUPS * SSD_STATE

IN_SIZES = (RWKV_CONV_CH, RWKV_W, DIFF_W, DIFF_W, DIFF_W, DIFF_W, SSD_W, SSD_XBC, 2 * SSD_HEADS, D_MODEL, D_MODEL, D_MODEL)
N_IN = sum(IN_SIZES)

kernel_name = "hybrid_rwkv7_diffattn_ssd_gated_dit"


def rms_norm(x, g, eps=EPS):
    xf = x.astype(jnp.float32)
    y = xf * lax.rsqrt(jnp.mean(xf * xf, axis=-1, keepdims=True) + eps)
    return (y * g.astype(jnp.float32)).astype(x.dtype)


def centred_dwconv(u, w):
    pad = w.shape[0] // 2
    return lax.conv_general_dilated(u, w[:, None, :].astype(u.dtype), (1,), [(pad, pad)],
                                    dimension_numbers=("NWC", "WIO", "NWC"),
                                    feature_group_count=u.shape[-1])


def modulate(x, mod, g):
    shift, scale, gate = jnp.split(mod, 3, axis=-1)
    return rms_norm(x, g) * (1.0 + scale) + shift, gate


def axial_rope(n_tok):
    rows = n_tok // GRID_W
    row = jnp.repeat(jnp.arange(rows, dtype=jnp.float32), GRID_W)
    col = jnp.tile(jnp.arange(GRID_W, dtype=jnp.float32), rows)
    n_freq = DIFF_HD // 4
    inv = ROPE_BASE ** (-jnp.arange(n_freq, dtype=jnp.float32) / n_freq)
    ang = jnp.concatenate([row[:, None] * inv, col[:, None] * inv], axis=-1)
    return jnp.cos(ang), jnp.sin(ang)


def apply_rope(t, cos, sin):
    half = t.shape[-1] // 2
    cos = cos[None, :, None, None, :].astype(t.dtype)
    sin = sin[None, :, None, None, :].astype(t.dtype)
    t1, t2 = t[..., :half], t[..., half:]
    return jnp.concatenate([t1 * cos - t2 * sin, t2 * cos + t1 * sin], axis=-1)


def rwkv_prep(u, conv_w, w0, w2, a0, a2, k_k, k_a):
    u = centred_dwconv(u, conv_w).astype(jnp.float32)
    b, n = u.shape[:2]
    r, k, v, lora = jnp.split(u, [RWKV_W, 2 * RWKV_W, 3 * RWKV_W], axis=-1)
    lora = lora.reshape(b, n, 2, 2, RWKV_LORA)
    w_raw = w0 + jnp.einsum("bndr,drc->bndc", jnp.tanh(lora[:, :, 0]), w2)
    decay = jnp.exp(-jnp.exp(-jax.nn.softplus(-w_raw) - 0.5))
    a = jax.nn.sigmoid(a0 + jnp.einsum("bndr,drc->bndc", lora[:, :, 1], a2))
    heads = lambda t: t.reshape(*t.shape[:-1], RWKV_HEADS, RWKV_HD)
    kk = heads(k * k_k)
    kk = kk / jnp.maximum(jnp.sqrt(jnp.sum(kk * kk, axis=-1, keepdims=True)), 1e-12)
    kk = kk.reshape(b, n, RWKV_W)
    k_dir = k[:, :, None] * (1.0 + (a - 1.0) * k_a)
    b_dir = kk[:, :, None] * a
    return heads(r), heads(decay), heads(k_dir), heads(v), heads(-kk), heads(b_dir)


def rwkv_scan(r, decay, k, v, a_vec, b_vec, s0, reverse):
    def step(S, inp):
        r_t, w_t, k_t, v_t, a_t, b_t = inp
        sa = jnp.einsum("bhvk,bhk->bhv", S, a_t)
        S = S * w_t[:, :, None, :] + sa[..., None] * b_t[:, :, None, :] + v_t[..., None] * k_t[:, :, None, :]
        return S, jnp.einsum("bhvk,bhk->bhv", S, r_t)
    xs = tuple(jnp.swapaxes(t, 0, 1) for t in (r, decay, k, v, a_vec, b_vec))
    s_fin, y = lax.scan(step, s0, xs, reverse=reverse)
    return jnp.swapaxes(y, 0, 1), s_fin


def rwkv_branch(uc, ul, gc, gl, conv_w, w0, w2, a0, a2, k_k, k_a, r_k, lnx_w, lnx_b):
    pc = rwkv_prep(uc, conv_w, w0, w2, a0, a2, k_k, k_a)
    pl = rwkv_prep(ul, conv_w, w0, w2, a0, a2, k_k, k_a)
    s0 = jnp.zeros((ul.shape[0], RWKV_HEADS, RWKV_HD, RWKV_HD), jnp.float32)

    def dir_inputs(p, d):
        r, dec, kd, v, av, bd = p
        return r, dec[:, :, d], kd[:, :, d], v, av, bd[:, :, d]

    yc_f, sc_f = rwkv_scan(*dir_inputs(pc, 0), s0, False)
    yc_b, sc_b = rwkv_scan(*dir_inputs(pc, 1), s0, True)
    yl_f, _ = rwkv_scan(*dir_inputs(pl, 0), sc_f, False)
    yl_b, _ = rwkv_scan(*dir_inputs(pl, 1), sc_b, True)

    def finish(y, p, g):
        r, _, kd, v, _, _ = p
        b, n = y.shape[:2]
        mu = jnp.mean(y, axis=-1, keepdims=True)
        var = jnp.mean(jnp.square(y - mu), axis=-1, keepdims=True)
        y = ((y - mu) * lax.rsqrt(var + RWKV_GN_EPS)).reshape(b, n, RWKV_W) * lnx_w + lnx_b
        bonus = jnp.sum(r * (kd[:, :, 0] + kd[:, :, 1]) * r_k.reshape(RWKV_HEADS, RWKV_HD), axis=-1, keepdims=True) * v
        out = (y + bonus.reshape(b, n, RWKV_W)) * jax.nn.silu(g.astype(jnp.float32))
        return out.astype(g.dtype)

    return finish(yc_f + yc_b, pc, gc), finish(yl_f + yl_b, pl, gl)


def diff_attention(qc, kc, vc, ql, kl, vl, lam):
    scale = DIFF_HD ** -0.5

    def attend(q, k, v):
        s = jnp.einsum("bqhjd,bkhjd->bhjqk", q, k).astype(jnp.float32) * scale
        p = jax.nn.softmax(s, axis=-1)
        a = (p[:, :, 0] - lam * p[:, :, 1]).astype(v.dtype)
        return jnp.einsum("bhqk,bkhe->bqhe", a, v)

    out_c = attend(qc, kc, vc)
    k_all = jnp.concatenate([kc, kl], axis=1)
    v_all = jnp.concatenate([vc, vl], axis=1)
    b, n = ql.shape[:2]
    q_blocks = jnp.swapaxes(ql.reshape(b, n // Q_BLOCK, Q_BLOCK, *ql.shape[2:]), 0, 1)
    out_blocks = lax.map(lambda qb: attend(qb, k_all, v_all), q_blocks)
    out_l = jnp.swapaxes(out_blocks, 0, 1).reshape(b, n, *vl.shape[2:])
    return out_c, out_l


def diff_branch(qc, kc, vc, gc, ql, kl, vl, gl, cos, sin, lam_p, subln_g, lam_init):
    lp = lam_p.astype(jnp.float32)
    lam = jnp.exp(jnp.sum(lp[0] * lp[1])) - jnp.exp(jnp.sum(lp[2] * lp[3])) + lam_init
    qk = lambda t: t.reshape(*t.shape[:2], DIFF_HEADS, 2, DIFF_HD)
    vh = lambda t: t.reshape(*t.shape[:2], DIFF_HEADS, DIFF_VD)
    oc, ol = diff_attention(qk(qc), qk(kc), vh(vc),
                            apply_rope(qk(ql), cos, sin), apply_rope(qk(kl), cos, sin), vh(vl), lam)

    def finish(o, g):
        o = rms_norm(o, subln_g) * (1.0 - lam_init)
        return o.reshape(g.shape) * jax.nn.silu(g)

    return finish(oc, gc), finish(ol, gl)


def ssd_chunked(x, dt, A, Bm, Cm, s0):
    b, l, h, p = x.shape
    g, n = Bm.shape[2], Bm.shape[3]
    e = h // g
    T = SSD_CHUNK
    nc = l // T
    xr = x.reshape(b, nc, T, g, e, p)
    dtr = dt.reshape(b, nc, T, g, e)
    Br = Bm.reshape(b, nc, T, g, n)
    Cr = Cm.reshape(b, nc, T, g, n)
    cum = jnp.cumsum(dtr * A.reshape(g, e), axis=2)
    xdt = xr * dtr[..., None]
    tri = jnp.tril(jnp.ones((T, T), dtype=bool))[None, None, :, :, None, None]
    seg = cum[:, :, :, None] - cum[:, :, None, :]
    lmat = jnp.exp(jnp.where(tri, seg, -jnp.inf))
    cb = jnp.einsum("bclgn,bcsgn->bclsg", Cr, Br)
    y_diag = jnp.einsum("bclsge,bcsgep->bclgep", cb[..., None] * lmat, xdt)
    decay_states = jnp.exp(cum[:, :, -1:] - cum)
    states = jnp.einsum("bcsgn,bcsgep->bcgepn", Br, xdt * decay_states[..., None])
    chunk_decay = jnp.exp(cum[:, :, -1])

    def step(S, inp):
        st, dec = inp
        return S * dec[..., None, None] + st, S

    s_fin, prev = lax.scan(step, s0.reshape(b, g, e, p, n),
                           (jnp.swapaxes(states, 0, 1), jnp.swapaxes(chunk_decay, 0, 1)))
    prev = jnp.swapaxes(prev, 0, 1)
    y_off = jnp.einsum("bclgn,bcgepn->bclgep", Cr, prev) * jnp.exp(cum)[..., None]
    return (y_diag + y_off).reshape(b, l, h, p), s_fin.reshape(b, h, p, n)


def ssd_prep(xbc, dt_raw, conv_w, conv_b, dt_bias):
    xbc = jax.nn.silu(centred_dwconv(xbc, conv_w) + conv_b).astype(jnp.float32)
    b, n = xbc.shape[:2]
    xs, bm, cm = jnp.split(xbc, [SSD_W, SSD_W + SSD_GROUPS * SSD_STATE], axis=-1)
    xs = xs.reshape(b, n, SSD_HEADS, SSD_HD)
    bm = bm.reshape(b, n, SSD_GROUPS, SSD_STATE)
    cm = cm.reshape(b, n, SSD_GROUPS, SSD_STATE)
    dt = jax.nn.softplus(dt_raw.astype(jnp.float32).reshape(b, n, 2, SSD_HEADS) + dt_bias)
    return xs, bm, cm, dt


def ssd_branch(zc, zl, xbc_c, xbc_l, dt_c, dt_l, conv_w, conv_b, a_log, dt_bias, d_skip, norm_g):
    A = -jnp.exp(a_log.astype(jnp.float32))
    xc, bc, cc, dc = ssd_prep(xbc_c, dt_c, conv_w, conv_b, dt_bias)
    xl, bl, cl, dl = ssd_prep(xbc_l, dt_l, conv_w, conv_b, dt_bias)
    s0 = jnp.zeros((xl.shape[0], SSD_HEADS, SSD_HD, SSD_STATE), jnp.float32)
    flip = lambda t: jnp.flip(t, axis=1)
    yc_f, sc_f = ssd_chunked(xc, dc[:, :, 0], A[0], bc, cc, s0)
    yl_f, _ = ssd_chunked(xl, dl[:, :, 0], A[0], bl, cl, sc_f)
    yc_b, sc_b = ssd_chunked(flip(xc), flip(dc[:, :, 1]), A[1], flip(bc), flip(cc), s0)
    yl_b, _ = ssd_chunked(flip(xl), flip(dl[:, :, 1]), A[1], flip(bl), flip(cl), sc_b)

    def finish(x, y, z):
        b, n = x.shape[:2]
        y = (y + d_skip[:, None] * x).reshape(b, n, SSD_W) * jax.nn.silu(z.astype(jnp.float32))
        yg = y.reshape(b, n, SSD_GROUPS, SSD_W // SSD_GROUPS)
        yg = yg * lax.rsqrt(jnp.mean(yg * yg, axis=-1, keepdims=True) + EPS)
        return (yg.reshape(b, n, SSD_W) * norm_g).astype(z.dtype)

    return finish(xc, yc_f + flip(yc_b), zc), finish(xl, yl_f + flip(yl_b), zl)


def setup_inputs(seed: int = 0) -> dict:
    key = jax.random.key(seed)
    ks = jax.random.split(key, 32)
    f32 = jnp.float32
    L, D = DEPTH, D_MODEL
    nrm = lambda k, shape, s: jax.random.normal(k, shape, f32) * s
    unif = lambda k, shape, lo, hi: jax.random.uniform(k, shape, f32, lo, hi)
    dt0 = jnp.exp(unif(ks[24], (L, 2, SSD_HEADS), math.log(1e-3), math.log(1e-1)))
    return {
        "x": nrm(ks[0], (BATCH, SEQ, D), 1.0),
        "c": nrm(ks[1], (BATCH, D), 1.0),
        "ctx": nrm(ks[2], (BATCH, CTX_LEN, D), 1.0),
        "c_ctx": nrm(ks[3], (D,), 1.0),
        "w_mod": nrm(ks[4], (L, D, 3 * D), D ** -0.5),
        "b_mod": nrm(ks[5], (L, 3 * D), 0.02),
        "g_pre": 1.0 + nrm(ks[6], (L, D), 0.02),
        "g_post": 1.0 + nrm(ks[7], (L, D), 0.02),
        "w_in": nrm(ks[8], (L, D, N_IN), D ** -0.5),
        "rwkv_conv": nrm(ks[9], (L, RWKV_SHIFT, RWKV_CONV_CH), 0.3).at[:, RWKV_SHIFT // 2].add(1.0),
        "rwkv_w0": unif(ks[10], (L, 2, RWKV_W), -6.0, 1.0),
        "rwkv_w2": nrm(ks[11], (L, 2, RWKV_LORA, RWKV_W), 0.1),
        "rwkv_a0": nrm(ks[12], (L, 2, RWKV_W), 0.1),
        "rwkv_a2": nrm(ks[13], (L, 2, RWKV_LORA, RWKV_W), 0.1),
        "rwkv_k_k": 1.0 + nrm(ks[14], (L, RWKV_W), 0.1),
        "rwkv_k_a": 1.0 + nrm(ks[15], (L, RWKV_W), 0.1),
        "rwkv_r_k": nrm(ks[16], (L, RWKV_W), 0.1),
        "rwkv_lnx_w": 1.0 + nrm(ks[17], (L, RWKV_W), 0.02),
        "rwkv_lnx_b": nrm(ks[18], (L, RWKV_W), 0.02),
        "diff_lambda": nrm(ks[19], (L, 4, DIFF_HD), 0.1),
        "diff_subln": 1.0 + nrm(ks[20], (L, DIFF_VD), 0.02),
        "ssd_conv_w": nrm(ks[21], (L, SSD_CONV, SSD_XBC), 0.3).at[:, SSD_CONV // 2].add(1.0),
        "ssd_conv_b": nrm(ks[22], (L, SSD_XBC), 0.02),
        "ssd_a_log": jnp.log(unif(ks[23], (L, 2, SSD_HEADS), 1.0, 16.0)),
        "ssd_dt_bias": dt0 + jnp.log(-jnp.expm1(-dt0)),
        "ssd_d": 1.0 + nrm(ks[25], (L, SSD_HEADS), 0.1),
        "ssd_norm": 1.0 + nrm(ks[26], (L, SSD_W), 0.02),
        "w_branch_a": nrm(ks[27], (L, RWKV_W, D), RWKV_W ** -0.5),
        "w_branch_b": nrm(ks[28], (L, DIFF_W, D), DIFF_W ** -0.5),
        "w_branch_c": nrm(ks[29], (L, SSD_W, D), SSD_W ** -0.5),
        "w_out": nrm(ks[30], (L, D, D), D ** -0.5),
    }


def reference(x, c, ctx, c_ctx, w_mod, b_mod, g_pre, g_post, w_in, rwkv_conv, rwkv_w0, rwkv_w2,
              rwkv_a0, rwkv_a2, rwkv_k_k, rwkv_k_a, rwkv_r_k, rwkv_lnx_w, rwkv_lnx_b, diff_lambda,
              diff_subln, ssd_conv_w, ssd_conv_b, ssd_a_log, ssd_dt_bias, ssd_d, ssd_norm,
              w_branch_a, w_branch_b, w_branch_c, w_out):
    cos, sin = axial_rope(x.shape[1])
    cond_l = jax.nn.silu(c)[:, None, :]
    cond_c = jax.nn.silu(c_ctx)[None, None, :]
    split_at = [int(i) for i in np.cumsum(IN_SIZES)[:-1]]
    xl, xc = x, ctx
    for li in range(DEPTH):
        lam_init = 0.8 - 0.6 * math.exp(-0.3 * li)
        hl, gate_l = modulate(xl, cond_l @ w_mod[li] + b_mod[li], g_pre[li])
        hc, gate_c = modulate(xc, cond_c @ w_mod[li] + b_mod[li], g_pre[li])
        ul = jnp.split(hl @ w_in[li], split_at, axis=-1)
        uc = jnp.split(hc @ w_in[li], split_at, axis=-1)
        oa_c, oa_l = rwkv_branch(uc[0], ul[0], uc[1], ul[1], rwkv_conv[li], rwkv_w0[li], rwkv_w2[li],
                                 rwkv_a0[li], rwkv_a2[li], rwkv_k_k[li], rwkv_k_a[li], rwkv_r_k[li],
                                 rwkv_lnx_w[li], rwkv_lnx_b[li])
        ob_c, ob_l = diff_branch(uc[2], uc[3], uc[4], uc[5], ul[2], ul[3], ul[4], ul[5], cos, sin,
                                 diff_lambda[li], diff_subln[li], lam_init)
        oc_c, oc_l = ssd_branch(uc[6], ul[6], uc[7], ul[7], uc[8], ul[8], ssd_conv_w[li], ssd_conv_b[li],
                                ssd_a_log[li], ssd_dt_bias[li], ssd_d[li], ssd_norm[li])

        def merge(oa, ob, oc, sa, sb, sc):
            m = (jax.nn.sigmoid(sa) * (oa @ w_branch_a[li]) + jax.nn.sigmoid(sb) * (ob @ w_branch_b[li])
                 + jax.nn.sigmoid(sc) * (oc @ w_branch_c[li]))
            return m @ w_out[li]

        out_l = merge(oa_l, ob_l, oc_l, ul[9], ul[10], ul[11])
        xl = xl + gate_l * rms_norm(out_l, g_post[li])
        if li < DEPTH - 1:
            out_c = merge(oa_c, ob_c, oc_c, uc[9], uc[10], uc[11])
            xc = xc + gate_c * rms_norm(out_c, g_post[li])
    return xl
```

```python
import functools
import math

import jax
import jax.numpy as jnp
from jax import lax
from jax.experimental import pallas as pl
from jax.experimental.pallas import tpu as pltpu

F32 = jnp.float32
BF16 = jnp.bfloat16

EPS = 1e-6
GRID_W = 64
ROPE_BASE = 10000.0

RWKV_HEADS = 16
RWKV_HD = 64
RWKV_W = RWKV_HEADS * RWKV_HD
RWKV_LORA = 64
RWKV_CONV_CH = 3 * RWKV_W + 4 * RWKV_LORA
RWKV_GN_EPS = 64e-5
RWKV_CHUNK = 64

DIFF_HEADS = 8
DIFF_HD = 64
DIFF_VD = 2 * DIFF_HD
DIFF_W = DIFF_HEADS * DIFF_VD

SSD_HEADS = 32
SSD_HD = 64
SSD_W = SSD_HEADS * SSD_HD
SSD_GROUPS = 8
SSD_STATE = 128
SSD_CHUNK = 128
SSD_XBC = SSD_W + 2 * SSD_GROUPS * SSD_STATE
SSD_GW = SSD_W // SSD_GROUPS

LANES = 128
VMEM_LIMIT = 56 * 1024 * 1024

NEG_BIG = -1e30


def _pick(n, candidates):
    for c in candidates:
        if n % c == 0:
            return c
    return n


def _mm(a, b):
    return jnp.dot(a.astype(BF16), b.astype(BF16), preferred_element_type=F32)


def _mm_nt(a, b):
    return lax.dot_general(a.astype(BF16), b.astype(BF16), (((1,), (1,)), ((), ())), preferred_element_type=F32)


def _mm_tn(a, b):
    return lax.dot_general(a.astype(BF16), b.astype(BF16), (((0,), (0,)), ((), ())), preferred_element_type=F32)


def _split3(x):
    x1 = x.astype(BF16)
    r1 = x - x1.astype(F32)
    x2 = r1.astype(BF16)
    x3 = (r1 - x2.astype(F32)).astype(BF16)
    return x1, x2, x3


def _mm_sel(sel, x):
    x1, x2, x3 = _split3(x)
    d = lambda y: jnp.dot(sel, y, preferred_element_type=F32)
    return d(x1) + d(x2) + d(x3)


def _softplus(x):
    return jnp.maximum(x, 0.0) + jnp.log(1.0 + jnp.exp(-jnp.abs(x)))


def _sigmoid(x):
    return 1.0 / (1.0 + jnp.exp(-x))


def _matmul_kernel(a_ref, w_ref, o_ref):
    o_ref[...] = _mm(a_ref[...], w_ref[...]).astype(o_ref.dtype)


def _matmul(a, w, out_dtype=F32, tm_cands=(768, 512, 384, 256, 128, 8), tn_cands=(512, 256, 128)):
    m, k = a.shape
    n = w.shape[1]
    tm = _pick(m, tm_cands)
    tn = _pick(n, tn_cands)
    return pl.pallas_call(
        _matmul_kernel,
        grid=(m // tm, n // tn),
        in_specs=[pl.BlockSpec((tm, k), lambda i, j: (i, 0)), pl.BlockSpec((k, tn), lambda i, j: (0, j))],
        out_specs=pl.BlockSpec((tm, tn), lambda i, j: (i, j)),
        out_shape=jax.ShapeDtypeStruct((m, n), out_dtype),
        compiler_params=pltpu.CompilerParams(dimension_semantics=("arbitrary", "arbitrary"),
                                             vmem_limit_bytes=VMEM_LIMIT),
        name="dense_matmul",
    )(a, w)


def _head_masks(rows):
    lane = lax.broadcasted_iota(jnp.int32, (rows, LANES), 1)
    return lane < RWKV_HD


def _bd(x, first):
    zero = jnp.zeros_like(x)
    return jnp.concatenate([jnp.where(first, x, zero), jnp.where(first, zero, x)], axis=0)


def _rwkv_local_kernel(r_ref, k_ref, v_ref, lw_ref, la_ref, w0_ref, w2_ref, a0_ref, a2_ref, kk_ref, ka_ref,
                       rk_ref, m_ref, n_ref, q_ref, yl_ref, bonus_ref):
    c = RWKV_CHUNK
    r = r_ref[0]
    k = k_ref[0]
    v = v_ref[0]
    lw = lw_ref[0]
    la = la_ref[0]
    first = _head_masks(c)
    first2 = _head_masks(2 * c)
    row = lax.broadcasted_iota(jnp.int32, (c, LANES), 0)
    col = lax.broadcasted_iota(jnp.int32, (c, LANES), 1) % RWKV_HD
    eye_cat = (row == col).astype(F32)
    hr = lax.broadcasted_iota(jnp.int32, (LANES, LANES), 0) // RWKV_HD
    hc = lax.broadcasted_iota(jnp.int32, (LANES, LANES), 1) // RWKV_HD
    head_ones = (hr == hc).astype(BF16)
    tr = lax.broadcasted_iota(jnp.int32, (c, c), 0)
    tc = lax.broadcasted_iota(jnp.int32, (c, c), 1)
    lane128 = lax.broadcasted_iota(jnp.int32, (c, LANES), 1)
    zrow = jnp.zeros((RWKV_LORA, LANES), F32)

    kkp = k * kk_ref[...]
    ss = _mm_sel_right(kkp * kkp, head_ones)
    kk = kkp / jnp.maximum(jnp.sqrt(ss), 1e-12)
    ka = ka_ref[...]
    a_sum = jnp.zeros_like(r)

    for d in (0, 1):
        if d == 0:
            incl, strict, tri = col <= row, col < row, (tc <= tr)
            w2s = jnp.concatenate([w2_ref[0], zrow], axis=0)
            a2s = jnp.concatenate([a2_ref[0], zrow], axis=0)
        else:
            incl, strict, tri = col >= row, col > row, (tc >= tr)
            w2s = jnp.concatenate([zrow, w2_ref[1]], axis=0)
            a2s = jnp.concatenate([zrow, a2_ref[1]], axis=0)
        w_raw = w0_ref[d:d + 1, :] + _mm(jnp.tanh(lw), w2s)
        logw = -jnp.exp(-_softplus(-w_raw) - 0.5)
        a_sig = _sigmoid(a0_ref[d:d + 1, :] + _mm(la, a2s))
        a_sum = a_sum + a_sig
        k_dir = k * (1.0 + (a_sig - 1.0) * ka)
        b_dir = kk * a_sig

        cum = _mm_sel(tri.astype(BF16), logw)
        tot = cum[c - 1:c, :] if d == 0 else cum[0:1, :]
        g_incl = jnp.exp(cum)
        g_excl = jnp.exp(cum - logw)
        g_inv = jnp.exp(-cum)
        g_hat = jnp.exp(tot - cum)
        at = -kk * g_excl
        rt = r * g_incl
        bt = b_dir * g_inv
        kt = k_dir * g_inv
        bh = b_dir * g_hat
        kh = k_dir * g_hat

        sc = _mm_nt(jnp.concatenate([at, rt], axis=0),
                    jnp.concatenate([_bd(bt, first), _bd(kt, first)], axis=0))
        zero = jnp.zeros((c, LANES), F32)
        a_ab = jnp.where(strict, sc[:c, :LANES], zero)
        a_ak = jnp.where(strict, sc[:c, LANES:], zero)
        a_rb = jnp.where(incl, sc[c:, :LANES], zero)
        a_rk = jnp.where(incl, sc[c:, LANES:], zero)

        p = a_ab
        t = eye_cat + p
        for _ in range(5):
            p = _mm(p, _bd(p, first))
            t = t + _mm(t, _bd(p, first))

        akv = _mm(a_ak, _bd(v, first))
        x = _mm(t, jnp.concatenate([_bd(akv, first), _bd(at, first)], axis=1))
        uloc = x[:, :LANES]
        w = x[:, LANES:]
        yloc = _mm(jnp.concatenate([a_rb, a_rk], axis=1),
                   jnp.concatenate([_bd(uloc, first), _bd(v, first)], axis=0))
        q = rt + _mm(a_rb, _bd(w, first))
        g = _mm_tn(jnp.concatenate([bh, kh], axis=1),
                   jnp.concatenate([uloc, w, v], axis=1))
        nfull = g[:LANES, :LANES] + g[LANES:, 2 * LANES:]
        mfull = g[:LANES, LANES:2 * LANES]
        zero = jnp.zeros((c, LANES), F32)
        n_cat = jnp.where(first, nfull[:c], zero) + jnp.where(first, zero, nfull[c:])
        m_cat = jnp.where(first, mfull[:c], zero) + jnp.where(first, zero, mfull[c:]) + eye_cat * jnp.exp(tot)

        m_ref[0, d, 0] = m_cat
        n_ref[0, d, 0] = n_cat
        q_ref[0, d] = q
        yl_ref[0, d] = yloc

    kd_sum = k * (2.0 + (a_sum - 2.0) * ka)
    bonus_ref[0] = _mm_sel_right(r * kd_sum * rk_ref[...], head_ones) * v


def _mm_sel_right(x, sel):
    x1, x2, x3 = _split3(x)
    d = lambda y: jnp.dot(y, sel, preferred_element_type=F32)
    return d(x1) + d(x2) + d(x3)


def _rwkv_local(uconv, w0, w2, a0, a2, k_k, k_a, r_k):
    b, n_all, _ = uconv.shape
    c = RWKV_CHUNK
    nc = n_all // c
    npair = RWKV_W // LANES
    row = lambda t: t.reshape(1, RWKV_W)
    col_blk = lambda off: pl.BlockSpec((1, c, LANES), lambda bi, ci, pi: (bi, ci, off + pi))
    fix_blk = lambda off: pl.BlockSpec((1, c, LANES), lambda bi, ci, pi: (bi, ci, off))
    vec2 = pl.BlockSpec((2, LANES), lambda bi, ci, pi: (0, pi))
    mat2 = pl.BlockSpec((2, RWKV_LORA, LANES), lambda bi, ci, pi: (0, 0, pi))
    vec1 = pl.BlockSpec((1, LANES), lambda bi, ci, pi: (0, pi))
    mn_shape = jax.ShapeDtypeStruct((b, 2, nc, RWKV_HD, RWKV_W), F32)
    mn_spec = pl.BlockSpec((1, 2, 1, RWKV_HD, LANES), lambda bi, ci, pi: (bi, 0, ci, 0, pi))
    qy_shape = jax.ShapeDtypeStruct((b, 2, n_all, RWKV_W), F32)
    qy_spec = pl.BlockSpec((1, 2, c, LANES), lambda bi, ci, pi: (bi, 0, ci, pi))
    return pl.pallas_call(
        _rwkv_local_kernel,
        grid=(b, nc, npair),
        in_specs=[col_blk(0), col_blk(npair), col_blk(2 * npair), fix_blk(3 * npair), fix_blk(3 * npair + 1),
                  vec2, mat2, vec2, mat2, vec1, vec1, vec1],
        out_specs=[mn_spec, mn_spec, qy_spec, qy_spec,
                   pl.BlockSpec((1, c, LANES), lambda bi, ci, pi: (bi, ci, pi))],
        out_shape=[mn_shape, mn_shape, qy_shape, qy_shape, jax.ShapeDtypeStruct((b, n_all, RWKV_W), F32)],
        compiler_params=pltpu.CompilerParams(dimension_semantics=("arbitrary",) * 3, vmem_limit_bytes=VMEM_LIMIT),
        name="rwkv_local",
    )(uconv, uconv, uconv, uconv, uconv, w0, w2, a0, a2, row(k_k), row(k_a), row(r_k))


def _rwkv_scan_kernel(m_ref, n_ref, q_ref, yl_ref, y_ref, s_ref):
    c = RWKV_CHUNK

    @pl.when(pl.program_id(2) == 0)
    def _():
        s_ref[...] = jnp.zeros_like(s_ref)

    first = _head_masks(RWKV_HD)
    for p in range(RWKV_W // LANES):
        sl = slice(p * LANES, (p + 1) * LANES)
        s1, s2, s3 = _split3(_bd(s_ref[:, sl], first))
        lhs = jnp.concatenate([q_ref[0, 0, :, sl], m_ref[0, 0, 0, :, sl]], axis=0)
        l1, l2, l3 = _split3(lhs)
        d = lambda x, y: jnp.dot(x, y, preferred_element_type=F32)
        z = d(l1, s1) + (d(l1, s2) + d(l2, s1)) + (d(l2, s2) + d(l1, s3) + d(l3, s1))
        y_ref[0, 0, :, sl] = yl_ref[0, 0, :, sl] + z[:c]
        s_ref[:, sl] = z[c:] + n_ref[0, 0, 0, :, sl]


def _chunk_order(d, i, n_ctx_chunks, n_chunks):
    back = jnp.where(i < n_ctx_chunks, n_ctx_chunks - 1 - i, n_chunks + n_ctx_chunks - 1 - i)
    return jnp.where(d == 0, i, back)


def _rwkv_scan(m, n, q, yl, n_ctx):
    b, _, nc, _, _ = m.shape
    n_all = q.shape[2]
    c = RWKV_CHUNK
    ncc = n_ctx // c
    cidx = lambda d, i: _chunk_order(d, i, ncc, nc)
    mn_spec = pl.BlockSpec((1, 1, 1, RWKV_HD, RWKV_W), lambda bi, d, i: (bi, d, cidx(d, i), 0, 0))
    qy_spec = pl.BlockSpec((1, 1, c, RWKV_W), lambda bi, d, i: (bi, d, cidx(d, i), 0))
    return pl.pallas_call(
        _rwkv_scan_kernel,
        grid=(b, 2, nc),
        in_specs=[mn_spec, mn_spec, qy_spec, qy_spec],
        out_specs=qy_spec,
        out_shape=jax.ShapeDtypeStruct((b, 2, n_all, RWKV_W), F32),
        scratch_shapes=[pltpu.VMEM((RWKV_HD, RWKV_W), F32)],
        compiler_params=pltpu.CompilerParams(dimension_semantics=("arbitrary",) * 3, vmem_limit_bytes=VMEM_LIMIT),
        name="rwkv_scan",
    )(m, n, q, yl)


def _diff_attn_kernel(lam_ref, q_ref, k_ref, v_ref, o_ref, qs_ref, m_ref, l_ref, acc_ref):
    tq = q_ref.shape[1]
    kv = pl.program_id(3)

    @pl.when(kv == 0)
    def _():
        q = q_ref[0]
        lane = lax.broadcasted_iota(jnp.int32, q.shape, 1)
        zero = jnp.zeros_like(q)
        qs_ref[...] = jnp.concatenate([jnp.where(lane < DIFF_HD, q, zero), jnp.where(lane < DIFF_HD, zero, q)], axis=0)
        m_ref[...] = jnp.full_like(m_ref, NEG_BIG)
        l_ref[...] = jnp.zeros_like(l_ref)
        acc_ref[...] = jnp.zeros_like(acc_ref)

    s = lax.dot_general(qs_ref[...], k_ref[0], (((1,), (1,)), ((), ())), preferred_element_type=F32)
    m_old = m_ref[...]
    m_new = jnp.maximum(m_old, jnp.max(s, axis=1, keepdims=True))
    alpha = jnp.exp(m_old - m_new)
    p = jnp.exp(s - m_new[:, :1])
    l_ref[...] = alpha * l_ref[...] + jnp.sum(p, axis=1, keepdims=True)
    acc_ref[...] = alpha * acc_ref[...] + jnp.dot(p.astype(BF16), v_ref[0], preferred_element_type=F32)
    m_ref[...] = m_new

    @pl.when(kv == pl.num_programs(3) - 1)
    def _():
        o = acc_ref[...] / l_ref[...]
        o_ref[0] = o[:tq] - lam_ref[0, 0] * o[tq:]


def _diff_attn(q, k, v, lam):
    b, nq, _ = q.shape
    nk = k.shape[1]
    tq = _pick(nq, (512, 256, 128))
    tk = _pick(nk, (768, 512, 256, 128))
    blk = lambda t, im: pl.BlockSpec((1, t, LANES), im)
    return pl.pallas_call(
        _diff_attn_kernel,
        grid=(b, DIFF_HEADS, nq // tq, nk // tk),
        in_specs=[pl.BlockSpec(memory_space=pltpu.SMEM),
                  blk(tq, lambda bi, h, i, j: (bi, i, h)),
                  blk(tk, lambda bi, h, i, j: (bi, j, h)),
                  blk(tk, lambda bi, h, i, j: (bi, j, h))],
        out_specs=blk(tq, lambda bi, h, i, j: (bi, i, h)),
        out_shape=jax.ShapeDtypeStruct((b, nq, DIFF_W), F32),
        scratch_shapes=[pltpu.VMEM((2 * tq, LANES), BF16), pltpu.VMEM((2 * tq, LANES), F32),
                        pltpu.VMEM((2 * tq, LANES), F32), pltpu.VMEM((2 * tq, LANES), F32)],
        compiler_params=pltpu.CompilerParams(dimension_semantics=("arbitrary",) * 4, vmem_limit_bytes=VMEM_LIMIT),
        name="diff_attn",
    )(lam.reshape(1, 1).astype(F32), q, k, v)


def _ssd_scan_kernel(x_ref, b_ref, c_ref, dt_ref, bias_ref, alog_ref, y_ref, st_ref):
    t = SSD_CHUNK
    d = pl.program_id(1)
    g = pl.program_id(2)
    e_heads = SSD_HEADS // SSD_GROUPS
    nh2 = 2 * SSD_HEADS

    @pl.when(pl.program_id(3) == 0)
    def _():
        st_ref[...] = jnp.zeros_like(st_ref)

    x = x_ref[0]
    bm = b_ref[0]
    cm = c_ref[0]
    dt = _softplus(dt_ref[0] + bias_ref[...])
    dta = dt * (-jnp.exp(alog_ref[...]))
    lane = lax.broadcasted_iota(jnp.int32, (t, nh2), 1)
    base = d * SSD_HEADS + g * e_heads
    lane_x = lax.broadcasted_iota(jnp.int32, (t, SSD_GW), 1) // SSD_HD
    lane_e = lax.broadcasted_iota(jnp.int32, (t, LANES), 1)
    dt_x = jnp.zeros((t, SSD_GW), F32)
    dta4 = jnp.zeros((t, LANES), F32)
    for e in range(e_heads):
        pick = lane == base + e
        dt_e = jnp.sum(jnp.where(pick, dt, 0.0), axis=1, keepdims=True)
        dta_e = jnp.sum(jnp.where(pick, dta, 0.0), axis=1, keepdims=True)
        dt_x = jnp.where(lane_x == e, dt_e, dt_x)
        dta4 = jnp.where(lane_e == e, dta_e, dta4)

    tr = lax.broadcasted_iota(jnp.int32, (t, t), 0)
    tc = lax.broadcasted_iota(jnp.int32, (t, t), 1)
    fwd = d == 0
    incl = jnp.where(fwd, tc - tr, tr - tc) <= 0
    cum4 = _mm_sel(incl.astype(BF16), dta4)
    cum_t = cum4.T
    last4 = jnp.where(fwd, cum4[t - 1:t, :], cum4[0:1, :])

    cum_x = jnp.zeros((t, SSD_GW), F32)
    last_x = jnp.zeros((1, SSD_GW), F32)
    for e in range(e_heads):
        cum_x = jnp.where(lane_x == e, cum4[:, e:e + 1], cum_x)
        last_x = jnp.where(lane_x[:1] == e, last4[:, e:e + 1], last_x)

    cb = _mm_nt(cm, bm)
    xdt = x * dt_x
    zero_h = jnp.zeros((t, LANES), F32)
    lane_h = lax.broadcasted_iota(jnp.int32, (t, LANES), 1) < SSD_HD
    ydiag = []
    for pair in range(e_heads // 2):
        lmats = []
        for e in (2 * pair, 2 * pair + 1):
            seg = cum4[:, e:e + 1] - cum_t[e:e + 1, :]
            lmats.append(cb * jnp.exp(jnp.where(incl, seg, NEG_BIG)))
        xp = xdt[:, pair * LANES:(pair + 1) * LANES]
        xbd = jnp.concatenate([jnp.where(lane_h, xp, zero_h), jnp.where(lane_h, zero_h, xp)], axis=0)
        ydiag.append(_mm(jnp.concatenate(lmats, axis=1), xbd))
    y_diag = jnp.concatenate(ydiag, axis=1)

    st = st_ref[...]
    y_off = _mm(cm, st) * jnp.exp(cum_x)
    y_ref[0, 0] = y_diag + y_off
    states = _mm_tn(bm, xdt * jnp.exp(last_x - cum_x))
    st_ref[...] = st * jnp.exp(last_x) + states


def _ssd_scan(xbc, dt_raw, dt_bias, a_log, n_ctx):
    b, n_all, _ = xbc.shape
    t = SSD_CHUNK
    nc = n_all // t
    ncc = n_ctx // t
    cidx = lambda d, i: _chunk_order(d, i, ncc, nc)
    nxg = SSD_W // SSD_GW
    nbg = SSD_W // SSD_STATE
    row = pl.BlockSpec((1, 2 * SSD_HEADS), lambda bi, d, g, i: (0, 0))
    return pl.pallas_call(
        _ssd_scan_kernel,
        grid=(b, 2, SSD_GROUPS, nc),
        in_specs=[pl.BlockSpec((1, t, SSD_GW), lambda bi, d, g, i: (bi, cidx(d, i), g)),
                  pl.BlockSpec((1, t, SSD_STATE), lambda bi, d, g, i: (bi, cidx(d, i), nbg + g)),
                  pl.BlockSpec((1, t, SSD_STATE), lambda bi, d, g, i: (bi, cidx(d, i), nbg + SSD_GROUPS + g)),
                  pl.BlockSpec((1, t, 2 * SSD_HEADS), lambda bi, d, g, i: (bi, cidx(d, i), 0)),
                  row, row],
        out_specs=pl.BlockSpec((1, 1, t, SSD_GW), lambda bi, d, g, i: (bi, d, cidx(d, i), g)),
        out_shape=jax.ShapeDtypeStruct((b, 2, n_all, SSD_W), F32),
        scratch_shapes=[pltpu.VMEM((SSD_STATE, SSD_GW), F32)],
        compiler_params=pltpu.CompilerParams(dimension_semantics=("arbitrary",) * 4, vmem_limit_bytes=VMEM_LIMIT),
        name="ssd_scan",
    )(xbc, xbc, xbc, dt_raw, dt_bias.reshape(1, -1), a_log.reshape(1, -1))


def _rms_norm(x, g, eps=EPS):
    return x * lax.rsqrt(jnp.mean(x * x, axis=-1, keepdims=True) + eps) * g


def _dwconv(u, w):
    width = w.shape[0]
    pad = width // 2
    n = u.shape[1]
    up = jnp.pad(u, ((0, 0), (pad, pad), (0, 0)))
    out = up[:, 0:n] * w[0]
    for i in range(1, width):
        out = out + up[:, i:i + n] * w[i]
    return out


def _dwconv_split(u, w, n_ctx):
    return jnp.concatenate([_dwconv(u[:, :n_ctx], w), _dwconv(u[:, n_ctx:], w)], axis=1)


def _axial_rope(n_tok):
    rows = n_tok // GRID_W
    row = jnp.repeat(jnp.arange(rows, dtype=F32), GRID_W)
    col = jnp.tile(jnp.arange(GRID_W, dtype=F32), rows)
    n_freq = DIFF_HD // 4
    inv = ROPE_BASE ** (-jnp.arange(n_freq, dtype=F32) / n_freq)
    ang = jnp.concatenate([row[:, None] * inv, col[:, None] * inv], axis=-1)
    return jnp.cos(ang), jnp.sin(ang)


def _apply_rope(t, cos, sin):
    b, n, _ = t.shape
    t5 = t.reshape(b, n, DIFF_HEADS, 2, DIFF_HD)
    half = DIFF_HD // 2
    c = cos[None, :, None, None, :]
    s = sin[None, :, None, None, :]
    t1, t2 = t5[..., :half], t5[..., half:]
    return jnp.concatenate([t1 * c - t2 * s, t2 * c + t1 * s], axis=-1).reshape(b, n, DIFF_W)


def _rwkv_branch(urw, g, n_ctx, conv_w, w0, w2, a0, a2, k_k, k_a, r_k, lnx_w, lnx_b):
    b, n_all, _ = urw.shape
    uconv = _dwconv_split(urw, conv_w, n_ctx)
    m, n, q, yl, bonus = _rwkv_local(uconv, w0, w2, a0, a2, k_k, k_a, r_k)
    y2 = _rwkv_scan(m, n, q, yl, n_ctx)
    y = (y2[:, 0] + y2[:, 1]).reshape(b, n_all, RWKV_HEADS, RWKV_HD)
    mu = jnp.mean(y, axis=-1, keepdims=True)
    var = jnp.mean(jnp.square(y - mu), axis=-1, keepdims=True)
    y = ((y - mu) * lax.rsqrt(var + RWKV_GN_EPS)).reshape(b, n_all, RWKV_W) * lnx_w + lnx_b
    return (y + bonus) * jax.nn.silu(g)


def _diff_branch(q, k, v, g, n_ctx, cos, sin, lam_p, subln_g, lam_init):
    b, n_all, _ = q.shape
    lam = jnp.exp(jnp.sum(lam_p[0] * lam_p[1])) - jnp.exp(jnp.sum(lam_p[2] * lam_p[3])) + lam_init
    scale = DIFF_HD ** -0.5
    qc, ql = q[:, :n_ctx], _apply_rope(q[:, n_ctx:], cos, sin)
    kc, kl = k[:, :n_ctx], _apply_rope(k[:, n_ctx:], cos, sin)
    k_all = jnp.concatenate([kc, kl], axis=1).astype(BF16)
    v_all = v.astype(BF16)
    oc = _diff_attn((qc * scale).astype(BF16), k_all[:, :n_ctx], v_all[:, :n_ctx], lam)
    ol = _diff_attn((ql * scale).astype(BF16), k_all, v_all, lam)
    o = jnp.concatenate([oc, ol], axis=1).reshape(b, n_all, DIFF_HEADS, DIFF_VD)
    o = _rms_norm(o, subln_g) * (1.0 - lam_init)
    return o.reshape(b, n_all, DIFF_W) * jax.nn.silu(g)


def _ssd_branch(z, xbc, dt_raw, n_ctx, conv_w, conv_b, a_log, dt_bias, d_skip, norm_g):
    b, n_all, _ = z.shape
    xa = jax.nn.silu(_dwconv_split(xbc, conv_w, n_ctx) + conv_b)
    y2 = _ssd_scan(xa, dt_raw, dt_bias, a_log, n_ctx)
    xs = xa[..., :SSD_W]
    y = y2[:, 0] + y2[:, 1] + jnp.repeat(d_skip, SSD_HD) * xs
    y = y * jax.nn.silu(z)
    yg = y.reshape(b, n_all, SSD_GROUPS, SSD_GW)
    yg = yg * lax.rsqrt(jnp.mean(yg * yg, axis=-1, keepdims=True) + EPS)
    return yg.reshape(b, n_all, SSD_W) * norm_g


def kernel(x, c, ctx, c_ctx, w_mod, b_mod, g_pre, g_post, w_in, rwkv_conv, rwkv_w0, rwkv_w2, rwkv_a0, rwkv_a2,
           rwkv_k_k, rwkv_k_a, rwkv_r_k, rwkv_lnx_w, rwkv_lnx_b, diff_lambda, diff_subln, ssd_conv_w, ssd_conv_b,
           ssd_a_log, ssd_dt_bias, ssd_d, ssd_norm, w_branch_a, w_branch_b, w_branch_c, w_out):
    b, n_lat, dm = x.shape
    n_ctx = ctx.shape[1]
    n_all = n_ctx + n_lat
    depth = w_in.shape[0]
    cos, sin = _axial_rope(n_lat)
    cond = jax.nn.silu(jnp.concatenate([c, c_ctx[None, :]], axis=0))
    cond = jnp.pad(cond, ((0, 8 - (b + 1)), (0, 0)))

    o_rw, o_g, o_dt = 0, RWKV_CONV_CH, RWKV_CONV_CH + RWKV_W
    pad_a = 192
    o_q = o_dt + 2 * SSD_HEADS + pad_a
    o_z = o_q + 4 * DIFF_W
    o_xbc = o_z + SSD_W
    o_gate = o_xbc + SSD_XBC
    src_dt = RWKV_CONV_CH + RWKV_W + 4 * DIFF_W + SSD_W + SSD_XBC

    xa = jnp.concatenate([ctx, x], axis=1)
    for li in range(depth):
        lam_init = 0.8 - 0.6 * math.exp(-0.3 * li)
        mod = _matmul(cond, w_mod[li], tn_cands=(1024, 512, 256, 128))[:b + 1] + b_mod[li]
        shift_l, scale_l, gate_l = jnp.split(mod[:b], 3, axis=-1)
        shift_c, scale_c, gate_c = jnp.split(mod[b], 3, axis=-1)
        is_ctx = (jnp.arange(n_all) < n_ctx)[None, :, None]
        sel = lambda vc, vl: jnp.where(is_ctx, vc[None, None, :], vl[:, None, :])
        h = _rms_norm(xa, g_pre[li]) * (1.0 + sel(scale_c, scale_l)) + sel(shift_c, shift_l)
        wl = w_in[li]
        w_cat = jnp.concatenate([wl[:, :o_dt], wl[:, src_dt:src_dt + 2 * SSD_HEADS],
                                 jnp.zeros((dm, pad_a), wl.dtype), wl[:, o_dt:src_dt],
                                 wl[:, src_dt + 2 * SSD_HEADS:]], axis=1).astype(BF16)
        u = _matmul(h.astype(BF16).reshape(b * n_all, dm), w_cat).reshape(b, n_all, -1)

        oa = _rwkv_branch(u[..., o_rw:o_g], u[..., o_g:o_dt], n_ctx, rwkv_conv[li], rwkv_w0[li], rwkv_w2[li],
                          rwkv_a0[li], rwkv_a2[li], rwkv_k_k[li], rwkv_k_a[li], rwkv_r_k[li], rwkv_lnx_w[li],
                          rwkv_lnx_b[li])
        ob = _diff_branch(u[..., o_q:o_q + DIFF_W], u[..., o_q + DIFF_W:o_q + 2 * DIFF_W],
                          u[..., o_q + 2 * DIFF_W:o_q + 3 * DIFF_W], u[..., o_q + 3 * DIFF_W:o_z], n_ctx, cos, sin,
                          diff_lambda[li], diff_subln[li], lam_init)
        oc = _ssd_branch(u[..., o_z:o_xbc], u[..., o_xbc:o_gate], u[..., o_dt:o_dt + 2 * SSD_HEADS], n_ctx,
                         ssd_conv_w[li], ssd_conv_b[li], ssd_a_log[li], ssd_dt_bias[li], ssd_d[li], ssd_norm[li])

        flat = lambda t: t.astype(BF16).reshape(b * n_all, -1)
        pa = _matmul(flat(oa), w_branch_a[li].astype(BF16)).reshape(b, n_all, dm)
        pb = _matmul(flat(ob), w_branch_b[li].astype(BF16)).reshape(b, n_all, dm)
        pc = _matmul(flat(oc), w_branch_c[li].astype(BF16)).reshape(b, n_all, dm)
        sa, sb, sc = u[..., o_gate:o_gate + dm], u[..., o_gate + dm:o_gate + 2 * dm], u[..., o_gate + 2 * dm:]
        mrg = jax.nn.sigmoid(sa) * pa + jax.nn.sigmoid(sb) * pb + jax.nn.sigmoid(sc) * pc
        out = _matmul(flat(mrg), w_out[li].astype(BF16)).reshape(b, n_all, dm)
        xa = xa + sel(gate_c, gate_l) * _rms_norm(out, g_post[li])
    return xa[:, n_ctx:]
```

```python
import functools
import math

import jax
import jax.numpy as jnp
from jax import lax
from jax.experimental import pallas as pl
from jax.experimental.pallas import tpu as pltpu

F32 = jnp.float32
BF16 = jnp.bfloat16

EPS = 1e-6
GRID_W = 64
ROPE_BASE = 10000.0

RWKV_HEADS = 16
RWKV_HD = 64
RWKV_W = RWKV_HEADS * RWKV_HD
RWKV_LORA = 64
RWKV_CONV_CH = 3 * RWKV_W + 4 * RWKV_LORA
RWKV_GN_EPS = 64e-5
RWKV_CHUNK = 64
RWKV_LOCAL_WIDTH = 1024

DIFF_HEADS = 8
DIFF_HD = 64
DIFF_VD = 2 * DIFF_HD
DIFF_W = DIFF_HEADS * DIFF_VD

SSD_HEADS = 32
SSD_HD = 64
SSD_W = SSD_HEADS * SSD_HD
SSD_GROUPS = 8
SSD_STATE = 128
SSD_CHUNK = 128
SSD_XBC = SSD_W + 2 * SSD_GROUPS * SSD_STATE
SSD_GW = SSD_W // SSD_GROUPS

LANES = 128
VMEM_LIMIT = 56 * 1024 * 1024

NEG_BIG = -1e30


def _pick(n, candidates):
    for c in candidates:
        if n % c == 0:
            return c
    return n


def _mm(a, b):
    return jnp.dot(a.astype(BF16), b.astype(BF16), preferred_element_type=F32)


def _mm_nt(a, b):
    return lax.dot_general(a.astype(BF16), b.astype(BF16), (((1,), (1,)), ((), ())), preferred_element_type=F32)


def _mm_tn(a, b):
    return lax.dot_general(a.astype(BF16), b.astype(BF16), (((0,), (0,)), ((), ())), preferred_element_type=F32)


def _split3(x):
    x1 = x.astype(BF16)
    r1 = x - x1.astype(F32)
    x2 = r1.astype(BF16)
    x3 = (r1 - x2.astype(F32)).astype(BF16)
    return x1, x2, x3


def _mm_sel(sel, x):
    x1, x2, x3 = _split3(x)
    d = lambda y: jnp.dot(sel, y, preferred_element_type=F32)
    return d(x1) + d(x2) + d(x3)


def _softplus(x):
    return jnp.maximum(x, 0.0) + jnp.log(1.0 + jnp.exp(-jnp.abs(x)))


def _sigmoid(x):
    return 1.0 / (1.0 + jnp.exp(-x))


def _matmul_kernel(a_ref, w_ref, o_ref):
    o_ref[...] = _mm(a_ref[...], w_ref[...]).astype(o_ref.dtype)


def _matmul(a, w, out_dtype=F32, tm_cands=(768, 512, 384, 256, 128, 8), tn_cands=(512, 256, 128)):
    m, k = a.shape
    n = w.shape[1]
    tm = _pick(m, tm_cands)
    tn = _pick(n, tn_cands)
    return pl.pallas_call(
        _matmul_kernel,
        grid=(m // tm, n // tn),
        in_specs=[pl.BlockSpec((tm, k), lambda i, j: (i, 0)), pl.BlockSpec((k, tn), lambda i, j: (0, j))],
        out_specs=pl.BlockSpec((tm, tn), lambda i, j: (i, j)),
        out_shape=jax.ShapeDtypeStruct((m, n), out_dtype),
        compiler_params=pltpu.CompilerParams(dimension_semantics=("arbitrary", "arbitrary"),
                                             vmem_limit_bytes=VMEM_LIMIT),
        name="dense_matmul",
    )(a, w)


def _head_masks(rows):
    lane = lax.broadcasted_iota(jnp.int32, (rows, LANES), 1)
    return lane < RWKV_HD


def _bd(x, first):
    zero = jnp.zeros_like(x)
    return jnp.concatenate([jnp.where(first, x, zero), jnp.where(first, zero, x)], axis=0)


def _rwkv_local_kernel(r_ref, k_ref, v_ref, lw_ref, la_ref, w0_ref, w2_ref, a0_ref, a2_ref, kk_ref, ka_ref,
                       rk_ref, m_ref, n_ref, q_ref, yl_ref, bonus_ref):
    c = RWKV_CHUNK
    width = r_ref.shape[2]
    first = _head_masks(c)
    row = lax.broadcasted_iota(jnp.int32, (c, LANES), 0)
    col = lax.broadcasted_iota(jnp.int32, (c, LANES), 1) % RWKV_HD
    eye_cat = (row == col).astype(F32)
    hr = lax.broadcasted_iota(jnp.int32, (LANES, LANES), 0) // RWKV_HD
    hc = lax.broadcasted_iota(jnp.int32, (LANES, LANES), 1) // RWKV_HD
    head_ones = (hr == hc).astype(BF16)
    tr = lax.broadcasted_iota(jnp.int32, (c, c), 0)
    tc = lax.broadcasted_iota(jnp.int32, (c, c), 1)
    zrow = jnp.zeros((RWKV_LORA, width), F32)

    tanh_lw = jnp.tanh(lw_ref[0])
    la = la_ref[0]
    w_raw2, a_sig2 = [], []
    for d in (0, 1):
        pad = (lambda m: jnp.concatenate([m, zrow], axis=0)) if d == 0 else (lambda m: jnp.concatenate([zrow, m], axis=0))
        w_raw2.append(w0_ref[d:d + 1, :] + _mm(tanh_lw, pad(w2_ref[d])))
        a_sig2.append(_sigmoid(a0_ref[d:d + 1, :] + _mm(la, pad(a2_ref[d]))))

    r_all, k_all, v_all = r_ref[0], k_ref[0], v_ref[0]
    kk_all, ka_all, rk_all = kk_ref[...], ka_ref[...], rk_ref[...]
    consts = (first, row, col, eye_cat, tr, tc)
    chains, outs = [], {}
    for pi in range(width // LANES):
        sl = slice(pi * LANES, (pi + 1) * LANES)
        r, k, v = r_all[:, sl], k_all[:, sl], v_all[:, sl]
        kkp = k * kk_all[:, sl]
        kk = kkp / jnp.maximum(jnp.sqrt(_mm_sel_right(kkp * kkp, head_ones)), 1e-12)
        a_sum = a_sig2[0][:, sl] + a_sig2[1][:, sl]
        kd_sum = k * (2.0 + (a_sum - 2.0) * ka_all[:, sl])
        outs[pi] = _mm_sel_right(r * kd_sum * rk_all[:, sl], head_ones) * v
        for d in (0, 1):
            chains.append(_rwkv_chain(d, r, k, v, kk, w_raw2[d][:, sl], a_sig2[d][:, sl], ka_all[:, sl], consts,
                                      outs, (pi, d)))
    while chains:
        alive = []
        for ch in chains:
            if next(ch, _DONE) is not _DONE:
                alive.append(ch)
        chains = alive
    for pi in range(width // LANES):
        sl = slice(pi * LANES, (pi + 1) * LANES)
        for d in (0, 1):
            m_cat, n_cat, q, yloc = outs[(pi, d)]
            m_ref[0, d, 0, :, sl] = m_cat
            n_ref[0, d, 0, :, sl] = n_cat
            q_ref[0, d, :, sl] = q
            yl_ref[0, d, :, sl] = yloc
        bonus_ref[0, :, sl] = outs[pi]


_DONE = object()


def _rwkv_chain(d, r, k, v, kk, w_raw, a_sig, ka, consts, outs, key):
    c = RWKV_CHUNK
    first, row, col, eye_cat, tr, tc = consts
    if d == 0:
        incl, strict, tri = col <= row, col < row, (tc <= tr)
    else:
        incl, strict, tri = col >= row, col > row, (tc >= tr)
    logw = -jnp.exp(-_softplus(-w_raw) - 0.5)
    k_dir = k * (1.0 + (a_sig - 1.0) * ka)
    b_dir = kk * a_sig

    cum = _mm_sel(tri.astype(BF16), logw)
    yield
    tot = cum[c - 1:c, :] if d == 0 else cum[0:1, :]
    g_incl = jnp.exp(cum)
    g_excl = jnp.exp(cum - logw)
    g_inv = jnp.exp(-cum)
    g_hat = jnp.exp(tot - cum)
    at = -kk * g_excl
    rt = r * g_incl
    bt = b_dir * g_inv
    kt = k_dir * g_inv
    bh = b_dir * g_hat
    kh = k_dir * g_hat

    sc = _mm_nt(jnp.concatenate([at, rt], axis=0),
                jnp.concatenate([_bd(bt, first), _bd(kt, first)], axis=0))
    yield
    zero = jnp.zeros((c, LANES), F32)
    a_ab = jnp.where(strict, sc[:c, :LANES], zero)
    a_ak = jnp.where(strict, sc[:c, LANES:], zero)
    a_rb = jnp.where(incl, sc[c:, :LANES], zero)
    a_rk = jnp.where(incl, sc[c:, LANES:], zero)

    t = eye_cat + a_ab
    p = _mm(a_ab, _bd(a_ab, first))
    akv = _mm(a_ak, _bd(v, first))
    yield
    for _ in range(4):
        pbd = _bd(p, first)
        t, p = t + _mm(t, pbd), _mm(p, pbd)
        yield
    t = t + _mm(t, _bd(p, first))
    yield

    x = _mm(t, jnp.concatenate([_bd(akv, first), _bd(at, first)], axis=1))
    yield
    uloc = x[:, :LANES]
    w = x[:, LANES:]
    yloc = _mm(jnp.concatenate([a_rb, a_rk], axis=1),
               jnp.concatenate([_bd(uloc, first), _bd(v, first)], axis=0))
    q = rt + _mm(a_rb, _bd(w, first))
    g = _mm_tn(jnp.concatenate([bh, kh], axis=1),
               jnp.concatenate([uloc, w, v], axis=1))
    yield
    nfull = g[:LANES, :LANES] + g[LANES:, 2 * LANES:]
    mfull = g[:LANES, LANES:2 * LANES]
    n_cat = jnp.where(first, nfull[:c], zero) + jnp.where(first, zero, nfull[c:])
    m_cat = jnp.where(first, mfull[:c], zero) + jnp.where(first, zero, mfull[c:]) + eye_cat * jnp.exp(tot)
    outs[key] = (m_cat, n_cat, q, yloc)


def _mm_sel_right(x, sel):
    x1, x2, x3 = _split3(x)
    d = lambda y: jnp.dot(y, sel, preferred_element_type=F32)
    return d(x1) + d(x2) + d(x3)


def _rwkv_local(uconv, w0, w2, a0, a2, k_k, k_a, r_k):
    b, n_all, _ = uconv.shape
    c = RWKV_CHUNK
    nc = n_all // c
    wb = RWKV_LOCAL_WIDTH
    nblk = RWKV_W // wb
    row = lambda t: t.reshape(1, RWKV_W)
    col_blk = lambda off: pl.BlockSpec((1, c, wb), lambda bi, ci, pi: (bi, ci, off + pi))
    lora_blk = lambda off: pl.BlockSpec((1, c, LANES), lambda bi, ci, pi: (bi, ci, off))
    vec2 = pl.BlockSpec((2, wb), lambda bi, ci, pi: (0, pi))
    mat2 = pl.BlockSpec((2, RWKV_LORA, wb), lambda bi, ci, pi: (0, 0, pi))
    vec1 = pl.BlockSpec((1, wb), lambda bi, ci, pi: (0, pi))
    mn_shape = jax.ShapeDtypeStruct((b, 2, nc, RWKV_HD, RWKV_W), F32)
    mn_spec = pl.BlockSpec((1, 2, 1, RWKV_HD, wb), lambda bi, ci, pi: (bi, 0, ci, 0, pi))
    qy_shape = jax.ShapeDtypeStruct((b, 2, n_all, RWKV_W), F32)
    qy_spec = pl.BlockSpec((1, 2, c, wb), lambda bi, ci, pi: (bi, 0, ci, pi))
    lora0 = 3 * RWKV_W // LANES
    return pl.pallas_call(
        _rwkv_local_kernel,
        grid=(b, nc, nblk),
        in_specs=[col_blk(0), col_blk(nblk), col_blk(2 * nblk), lora_blk(lora0), lora_blk(lora0 + 1),
                  vec2, mat2, vec2, mat2, vec1, vec1, vec1],
        out_specs=[mn_spec, mn_spec, qy_spec, qy_spec,
                   pl.BlockSpec((1, c, wb), lambda bi, ci, pi: (bi, ci, pi))],
        out_shape=[mn_shape, mn_shape, qy_shape, qy_shape, jax.ShapeDtypeStruct((b, n_all, RWKV_W), F32)],
        compiler_params=pltpu.CompilerParams(dimension_semantics=("arbitrary",) * 3, vmem_limit_bytes=VMEM_LIMIT),
        name="rwkv_local",
    )(uconv, uconv, uconv, uconv, uconv, w0, w2, a0, a2, row(k_k), row(k_a), row(r_k))


def _rwkv_scan_kernel(m_ref, n_ref, q_ref, yl_ref, y_ref, s_ref):
    c = RWKV_CHUNK

    @pl.when(pl.program_id(2) == 0)
    def _():
        s_ref[...] = jnp.zeros_like(s_ref)

    first = _head_masks(RWKV_HD)
    for p in range(RWKV_W // LANES):
        sl = slice(p * LANES, (p + 1) * LANES)
        s1, s2, s3 = _split3(_bd(s_ref[:, sl], first))
        lhs = jnp.concatenate([q_ref[0, 0, :, sl], m_ref[0, 0, 0, :, sl]], axis=0)
        l1, l2, l3 = _split3(lhs)
        d = lambda x, y: jnp.dot(x, y, preferred_element_type=F32)
        z = d(l1, s1) + (d(l1, s2) + d(l2, s1)) + (d(l2, s2) + d(l1, s3) + d(l3, s1))
        y_ref[0, 0, :, sl] = yl_ref[0, 0, :, sl] + z[:c]
        s_ref[:, sl] = z[c:] + n_ref[0, 0, 0, :, sl]


def _chunk_order(d, i, n_ctx_chunks, n_chunks):
    back = jnp.where(i < n_ctx_chunks, n_ctx_chunks - 1 - i, n_chunks + n_ctx_chunks - 1 - i)
    return jnp.where(d == 0, i, back)


def _rwkv_scan(m, n, q, yl, n_ctx):
    b, _, nc, _, _ = m.shape
    n_all = q.shape[2]
    c = RWKV_CHUNK
    ncc = n_ctx // c
    cidx = lambda d, i: _chunk_order(d, i, ncc, nc)
    mn_spec = pl.BlockSpec((1, 1, 1, RWKV_HD, RWKV_W), lambda bi, d, i: (bi, d, cidx(d, i), 0, 0))
    qy_spec = pl.BlockSpec((1, 1, c, RWKV_W), lambda bi, d, i: (bi, d, cidx(d, i), 0))
    return pl.pallas_call(
        _rwkv_scan_kernel,
        grid=(b, 2, nc),
        in_specs=[mn_spec, mn_spec, qy_spec, qy_spec],
        out_specs=qy_spec,
        out_shape=jax.ShapeDtypeStruct((b, 2, n_all, RWKV_W), F32),
        scratch_shapes=[pltpu.VMEM((RWKV_HD, RWKV_W), F32)],
        compiler_params=pltpu.CompilerParams(dimension_semantics=("arbitrary",) * 3, vmem_limit_bytes=VMEM_LIMIT),
        name="rwkv_scan",
    )(m, n, q, yl)


ATTN_COLS = 256


def _diff_attn_kernel(lam_ref, qt_ref, k_ref, vt_ref, o_ref, qs_ref, m_ref, l_ref, acc_ref):
    tq = qt_ref.shape[2]
    tk = k_ref.shape[1]
    kv = pl.program_id(3)

    @pl.when(kv == 0)
    def _():
        qt = qt_ref[0]
        feat = lax.broadcasted_iota(jnp.int32, qt.shape, 0)
        zero = jnp.zeros_like(qt)
        qs_ref[...] = jnp.concatenate([jnp.where(feat < DIFF_HD, qt, zero), jnp.where(feat < DIFF_HD, zero, qt)], axis=1)
        m_ref[...] = jnp.full_like(m_ref, NEG_BIG)
        l_ref[...] = jnp.zeros_like(l_ref)
        acc_ref[...] = jnp.zeros_like(acc_ref)

    k = k_ref[0]
    v_ext = jnp.concatenate([vt_ref[0], jnp.ones((ATTN_ONES_ROWS, tk), BF16)], axis=0)
    strips = [slice(cb * ATTN_COLS, (cb + 1) * ATTN_COLS) for cb in range(2 * tq // ATTN_COLS)]
    scores = [jnp.dot(k, qs_ref[:, sl], preferred_element_type=F32) for sl in strips]
    for sl, s in zip(strips, scores):
        m_old = m_ref[:, sl]
        m_new = jnp.maximum(m_old, jnp.max(s, axis=0, keepdims=True))
        alpha = jnp.exp2(m_old - m_new)
        p = jnp.exp2(s - m_new[:1]).astype(BF16)
        pv = jnp.dot(v_ext, p, preferred_element_type=F32)
        acc_ref[:, sl] = alpha[:1] * acc_ref[:, sl] + pv[:DIFF_VD]
        l_ref[:, sl] = alpha * l_ref[:, sl] + pv[DIFF_VD:DIFF_VD + 8]
        m_ref[:, sl] = m_new

    @pl.when(kv == pl.num_programs(3) - 1)
    def _():
        o = acc_ref[...] / l_ref[:1, :]
        o_ref[0] = o[:, :tq] - lam_ref[0, 0] * o[:, tq:]


ATTN_ONES_ROWS = 16


def _diff_attn(qt, k, vt, lam):
    b, _, nq = qt.shape
    nk = k.shape[1]
    tq = _pick(nq, (512, 256, 128))
    tk = _pick(nk, (768, 512, 256, 128))
    return pl.pallas_call(
        _diff_attn_kernel,
        grid=(b, DIFF_HEADS, nq // tq, nk // tk),
        in_specs=[pl.BlockSpec(memory_space=pltpu.SMEM),
                  pl.BlockSpec((1, LANES, tq), lambda bi, h, i, j: (bi, h, i)),
                  pl.BlockSpec((1, tk, LANES), lambda bi, h, i, j: (bi, j, h)),
                  pl.BlockSpec((1, LANES, tk), lambda bi, h, i, j: (bi, h, j))],
        out_specs=pl.BlockSpec((1, LANES, tq), lambda bi, h, i, j: (bi, h, i)),
        out_shape=jax.ShapeDtypeStruct((b, DIFF_W, nq), F32),
        scratch_shapes=[pltpu.VMEM((LANES, 2 * tq), BF16), pltpu.VMEM((8, 2 * tq), F32),
                        pltpu.VMEM((8, 2 * tq), F32), pltpu.VMEM((DIFF_VD, 2 * tq), F32)],
        compiler_params=pltpu.CompilerParams(dimension_semantics=("arbitrary",) * 4, vmem_limit_bytes=VMEM_LIMIT),
        name="diff_attn",
    )(lam.reshape(1, 1).astype(F32), qt, k, vt)


SSD_GROUPS_PER_STEP = 4


def _ssd_scan_kernel(x_ref, b_ref, c_ref, dt_ref, bias_ref, alog_ref, y_ref, st_ref):
    t = SSD_CHUNK
    d = pl.program_id(1)
    gstep = pl.program_id(2)
    ngrp = x_ref.shape[2] // SSD_GW

    @pl.when(pl.program_id(3) == 0)
    def _():
        st_ref[...] = jnp.zeros_like(st_ref)

    dt = _softplus(dt_ref[0] + bias_ref[...])
    dta = dt * (-jnp.exp(alog_ref[...]))
    tr = lax.broadcasted_iota(jnp.int32, (t, t), 0)
    tc = lax.broadcasted_iota(jnp.int32, (t, t), 1)
    fwd = d == 0
    incl = jnp.where(fwd, tc - tr, tr - tc) <= 0
    x_all, b_all, c_all, st_all = x_ref[0], b_ref[0], c_ref[0], st_ref[...]
    outs = {}
    chains = []
    for j in range(ngrp):
        xs = slice(j * SSD_GW, (j + 1) * SSD_GW)
        ns = slice(j * SSD_STATE, (j + 1) * SSD_STATE)
        first_head = d * SSD_HEADS + (gstep * ngrp + j) * (SSD_HEADS // SSD_GROUPS)
        chains.append(_ssd_chain(x_all[:, xs], b_all[:, ns], c_all[:, ns], st_all[:, xs], dt, dta, first_head, fwd,
                                 incl, outs, j))
    while chains:
        alive = []
        for ch in chains:
            if next(ch, _DONE) is not _DONE:
                alive.append(ch)
        chains = alive
    for j in range(ngrp):
        xs = slice(j * SSD_GW, (j + 1) * SSD_GW)
        y, st_new = outs[j]
        y_ref[0, 0, :, xs] = y
        st_ref[:, xs] = st_new


def _ssd_chain(x, bm, cm, st, dt, dta, first_head, fwd, incl, outs, key):
    t = SSD_CHUNK
    e_heads = SSD_HEADS // SSD_GROUPS
    lane = lax.broadcasted_iota(jnp.int32, (t, 2 * SSD_HEADS), 1)
    lane_x = lax.broadcasted_iota(jnp.int32, (t, SSD_GW), 1) // SSD_HD
    lane_e = lax.broadcasted_iota(jnp.int32, (t, LANES), 1)
    dt_x = jnp.zeros((t, SSD_GW), F32)
    dta4 = jnp.zeros((t, LANES), F32)
    for e in range(e_heads):
        pick = lane == first_head + e
        dt_e = jnp.sum(jnp.where(pick, dt, 0.0), axis=1, keepdims=True)
        dta_e = jnp.sum(jnp.where(pick, dta, 0.0), axis=1, keepdims=True)
        dt_x = jnp.where(lane_x == e, dt_e, dt_x)
        dta4 = jnp.where(lane_e == e, dta_e, dta4)

    cum4 = _mm_sel(incl.astype(BF16), dta4)
    cb = _mm_nt(cm, bm)
    y_off_raw = _mm(cm, st)
    yield
    cum_t = cum4.T
    last4 = jnp.where(fwd, cum4[t - 1:t, :], cum4[0:1, :])
    cum_x = jnp.zeros((t, SSD_GW), F32)
    last_x = jnp.zeros((1, SSD_GW), F32)
    for e in range(e_heads):
        cum_x = jnp.where(lane_x == e, cum4[:, e:e + 1], cum_x)
        last_x = jnp.where(lane_x[:1] == e, last4[:, e:e + 1], last_x)

    xdt = x * dt_x
    zero_h = jnp.zeros((t, LANES), F32)
    lane_h = lax.broadcasted_iota(jnp.int32, (t, LANES), 1) < SSD_HD
    ydiag = []
    for pair in range(e_heads // 2):
        lmats = []
        for e in (2 * pair, 2 * pair + 1):
            seg = cum4[:, e:e + 1] - cum_t[e:e + 1, :]
            lmats.append(cb * jnp.exp(jnp.where(incl, seg, NEG_BIG)))
        xp = xdt[:, pair * LANES:(pair + 1) * LANES]
        xbd = jnp.concatenate([jnp.where(lane_h, xp, zero_h), jnp.where(lane_h, zero_h, xp)], axis=0)
        ydiag.append(_mm(jnp.concatenate(lmats, axis=1), xbd))
    states = _mm_tn(bm, xdt * jnp.exp(last_x - cum_x))
    yield
    y = jnp.concatenate(ydiag, axis=1) + y_off_raw * jnp.exp(cum_x)
    outs[key] = (y, st * jnp.exp(last_x) + states)


def _ssd_scan(xbc, dt_raw, dt_bias, a_log, n_ctx):
    b, n_all, _ = xbc.shape
    t = SSD_CHUNK
    nc = n_all // t
    ncc = n_ctx // t
    cidx = lambda d, i: _chunk_order(d, i, ncc, nc)
    gg = SSD_GROUPS_PER_STEP
    nsteps_g = SSD_GROUPS // gg
    xw, nw = gg * SSD_GW, gg * SSD_STATE
    b0 = SSD_W // nw
    row = pl.BlockSpec((1, 2 * SSD_HEADS), lambda bi, d, g, i: (0, 0))
    return pl.pallas_call(
        _ssd_scan_kernel,
        grid=(b, 2, nsteps_g, nc),
        in_specs=[pl.BlockSpec((1, t, xw), lambda bi, d, g, i: (bi, cidx(d, i), g)),
                  pl.BlockSpec((1, t, nw), lambda bi, d, g, i: (bi, cidx(d, i), b0 + g)),
                  pl.BlockSpec((1, t, nw), lambda bi, d, g, i: (bi, cidx(d, i), b0 + nsteps_g + g)),
                  pl.BlockSpec((1, t, 2 * SSD_HEADS), lambda bi, d, g, i: (bi, cidx(d, i), 0)),
                  row, row],
        out_specs=pl.BlockSpec((1, 1, t, xw), lambda bi, d, g, i: (bi, d, cidx(d, i), g)),
        out_shape=jax.ShapeDtypeStruct((b, 2, n_all, SSD_W), F32),
        scratch_shapes=[pltpu.VMEM((SSD_STATE, xw), F32)],
        compiler_params=pltpu.CompilerParams(dimension_semantics=("arbitrary",) * 4, vmem_limit_bytes=VMEM_LIMIT),
        name="ssd_scan",
    )(xbc, xbc, xbc, dt_raw, dt_bias.reshape(1, -1), a_log.reshape(1, -1))


def _rms_norm(x, g, eps=EPS):
    return x * lax.rsqrt(jnp.mean(x * x, axis=-1, keepdims=True) + eps) * g


def _dwconv(u, w):
    width = w.shape[0]
    pad = width // 2
    n = u.shape[1]
    up = jnp.pad(u, ((0, 0), (pad, pad), (0, 0)))
    out = up[:, 0:n] * w[0]
    for i in range(1, width):
        out = out + up[:, i:i + n] * w[i]
    return out


def _dwconv_split(u, w, n_ctx):
    return jnp.concatenate([_dwconv(u[:, :n_ctx], w), _dwconv(u[:, n_ctx:], w)], axis=1)


def _axial_rope(n_tok):
    rows = n_tok // GRID_W
    row = jnp.repeat(jnp.arange(rows, dtype=F32), GRID_W)
    col = jnp.tile(jnp.arange(GRID_W, dtype=F32), rows)
    n_freq = DIFF_HD // 4
    inv = ROPE_BASE ** (-jnp.arange(n_freq, dtype=F32) / n_freq)
    ang = jnp.concatenate([row[:, None] * inv, col[:, None] * inv], axis=-1)
    return jnp.cos(ang), jnp.sin(ang)


def _apply_rope(t, cos, sin):
    b, n, _ = t.shape
    t5 = t.reshape(b, n, DIFF_HEADS, 2, DIFF_HD)
    half = DIFF_HD // 2
    c = cos[None, :, None, None, :]
    s = sin[None, :, None, None, :]
    t1, t2 = t5[..., :half], t5[..., half:]
    return jnp.concatenate([t1 * c - t2 * s, t2 * c + t1 * s], axis=-1).reshape(b, n, DIFF_W)


def _rwkv_branch(urw, g, n_ctx, conv_w, w0, w2, a0, a2, k_k, k_a, r_k, lnx_w, lnx_b):
    b, n_all, _ = urw.shape
    uconv = _dwconv_split(urw, conv_w, n_ctx)
    m, n, q, yl, bonus = _rwkv_local(uconv, w0, w2, a0, a2, k_k, k_a, r_k)
    y2 = _rwkv_scan(m, n, q, yl, n_ctx)
    y = (y2[:, 0] + y2[:, 1]).reshape(b, n_all, RWKV_HEADS, RWKV_HD)
    mu = jnp.mean(y, axis=-1, keepdims=True)
    var = jnp.mean(jnp.square(y - mu), axis=-1, keepdims=True)
    y = ((y - mu) * lax.rsqrt(var + RWKV_GN_EPS)).reshape(b, n_all, RWKV_W) * lnx_w + lnx_b
    return (y + bonus) * jax.nn.silu(g)


def _diff_branch(q, k, v, g, n_ctx, cos, sin, lam_p, subln_g, lam_init):
    b, n_all, _ = q.shape
    lam = jnp.exp(jnp.sum(lam_p[0] * lam_p[1])) - jnp.exp(jnp.sum(lam_p[2] * lam_p[3])) + lam_init
    scale = DIFF_HD ** -0.5 * math.log2(math.e)
    qc, ql = q[:, :n_ctx], _apply_rope(q[:, n_ctx:], cos, sin)
    kc, kl = k[:, :n_ctx], _apply_rope(k[:, n_ctx:], cos, sin)
    k_all = jnp.concatenate([kc, kl], axis=1).astype(BF16)
    vt_all = jnp.swapaxes(v, 1, 2).astype(BF16)
    feat_major = lambda t: jnp.swapaxes(t * scale, 1, 2).astype(BF16)
    oc = _diff_attn(feat_major(qc), k_all[:, :n_ctx], vt_all[:, :, :n_ctx], lam)
    ol = _diff_attn(feat_major(ql), k_all, vt_all, lam)
    o = jnp.swapaxes(jnp.concatenate([oc, ol], axis=2), 1, 2).reshape(b, n_all, DIFF_HEADS, DIFF_VD)
    o = _rms_norm(o, subln_g) * (1.0 - lam_init)
    return o.reshape(b, n_all, DIFF_W) * jax.nn.silu(g)


def _ssd_branch(z, xbc, dt_raw, n_ctx, conv_w, conv_b, a_log, dt_bias, d_skip, norm_g):
    b, n_all, _ = z.shape
    xa = jax.nn.silu(_dwconv_split(xbc, conv_w, n_ctx) + conv_b)
    y2 = _ssd_scan(xa, dt_raw, dt_bias, a_log, n_ctx)
    xs = xa[..., :SSD_W]
    y = y2[:, 0] + y2[:, 1] + jnp.repeat(d_skip, SSD_HD) * xs
    y = y * jax.nn.silu(z)
    yg = y.reshape(b, n_all, SSD_GROUPS, SSD_GW)
    yg = yg * lax.rsqrt(jnp.mean(yg * yg, axis=-1, keepdims=True) + EPS)
    return yg.reshape(b, n_all, SSD_W) * norm_g


def kernel(x, c, ctx, c_ctx, w_mod, b_mod, g_pre, g_post, w_in, rwkv_conv, rwkv_w0, rwkv_w2, rwkv_a0, rwkv_a2,
           rwkv_k_k, rwkv_k_a, rwkv_r_k, rwkv_lnx_w, rwkv_lnx_b, diff_lambda, diff_subln, ssd_conv_w, ssd_conv_b,
           ssd_a_log, ssd_dt_bias, ssd_d, ssd_norm, w_branch_a, w_branch_b, w_branch_c, w_out):
    b, n_lat, dm = x.shape
    n_ctx = ctx.shape[1]
    n_all = n_ctx + n_lat
    depth = w_in.shape[0]
    cos, sin = _axial_rope(n_lat)
    cond = jax.nn.silu(jnp.concatenate([c, c_ctx[None, :]], axis=0))
    cond = jnp.pad(cond, ((0, 8 - (b + 1)), (0, 0)))

    o_rw, o_g, o_dt = 0, RWKV_CONV_CH, RWKV_CONV_CH + RWKV_W
    src_dt = o_dt + 4 * DIFF_W + SSD_W + SSD_XBC
    pad_a = -(o_dt + 2 * SSD_HEADS) % 512

    xa = jnp.concatenate([ctx, x], axis=1)
    for li in range(depth):
        lam_init = 0.8 - 0.6 * math.exp(-0.3 * li)
        mod = _matmul(cond, w_mod[li], tn_cands=(1024, 512, 256, 128))[:b + 1] + b_mod[li]
        shift_l, scale_l, gate_l = jnp.split(mod[:b], 3, axis=-1)
        shift_c, scale_c, gate_c = jnp.split(mod[b], 3, axis=-1)
        is_ctx = (jnp.arange(n_all) < n_ctx)[None, :, None]
        sel = lambda vc, vl: jnp.where(is_ctx, vc[None, None, :], vl[:, None, :])
        h = _rms_norm(xa, g_pre[li]) * (1.0 + sel(scale_c, scale_l)) + sel(shift_c, shift_l)
        wl = w_in[li]
        hb = h.astype(BF16).reshape(b * n_all, dm)
        proj = lambda w: _matmul(hb, w.astype(BF16)).reshape(b, n_all, -1)
        ua = proj(jnp.concatenate([wl[:, :o_dt], wl[:, src_dt:src_dt + 2 * SSD_HEADS],
                                   jnp.zeros((dm, pad_a), wl.dtype)], axis=1))
        ub = proj(wl[:, o_dt:o_dt + 4 * DIFF_W])
        uc = proj(wl[:, o_dt + 4 * DIFF_W:src_dt])
        ug = proj(wl[:, src_dt + 2 * SSD_HEADS:])

        oa = _rwkv_branch(ua[..., o_rw:o_g], ua[..., o_g:o_dt], n_ctx, rwkv_conv[li], rwkv_w0[li], rwkv_w2[li],
                          rwkv_a0[li], rwkv_a2[li], rwkv_k_k[li], rwkv_k_a[li], rwkv_r_k[li], rwkv_lnx_w[li],
                          rwkv_lnx_b[li])
        ob = _diff_branch(ub[..., :DIFF_W], ub[..., DIFF_W:2 * DIFF_W], ub[..., 2 * DIFF_W:3 * DIFF_W],
                          ub[..., 3 * DIFF_W:], n_ctx, cos, sin, diff_lambda[li], diff_subln[li], lam_init)
        oc = _ssd_branch(uc[..., :SSD_W], uc[..., SSD_W:], ua[..., o_dt:o_dt + 2 * SSD_HEADS], n_ctx,
                         ssd_conv_w[li], ssd_conv_b[li], ssd_a_log[li], ssd_dt_bias[li], ssd_d[li], ssd_norm[li])

        flat = lambda t: t.astype(BF16).reshape(b * n_all, -1)
        pa = _matmul(flat(oa), w_branch_a[li].astype(BF16)).reshape(b, n_all, dm)
        pb = _matmul(flat(ob), w_branch_b[li].astype(BF16)).reshape(b, n_all, dm)
        pc = _matmul(flat(oc), w_branch_c[li].astype(BF16)).reshape(b, n_all, dm)
        sa, sb, sc = ug[..., :dm], ug[..., dm:2 * dm], ug[..., 2 * dm:]
        mrg = jax.nn.sigmoid(sa) * pa + jax.nn.sigmoid(sb) * pb + jax.nn.sigmoid(sc) * pc
        out = _matmul(flat(mrg), w_out[li].astype(BF16)).reshape(b, n_all, dm)
        xa = xa + sel(gate_c, gate_l) * _rms_norm(out, g_post[li])
    return xa[:, n_ctx:]
```

```python
import functools
import math

import jax
import jax.numpy as jnp
from jax import lax
from jax.experimental import pallas as pl
from jax.experimental.pallas import tpu as pltpu

F32 = jnp.float32
BF16 = jnp.bfloat16

EPS = 1e-6
GRID_W = 64
ROPE_BASE = 10000.0

RWKV_HEADS = 16
RWKV_HD = 64
RWKV_W = RWKV_HEADS * RWKV_HD
RWKV_LORA = 64
RWKV_CONV_CH = 3 * RWKV_W + 4 * RWKV_LORA
RWKV_GN_EPS = 64e-5
RWKV_CHUNK = 64
RWKV_LOCAL_WIDTH = 1024

DIFF_HEADS = 8
DIFF_HD = 64
DIFF_VD = 2 * DIFF_HD
DIFF_W = DIFF_HEADS * DIFF_VD

SSD_HEADS = 32
SSD_HD = 64
SSD_W = SSD_HEADS * SSD_HD
SSD_GROUPS = 8
SSD_STATE = 128
SSD_CHUNK = 128
SSD_XBC = SSD_W + 2 * SSD_GROUPS * SSD_STATE
SSD_GW = SSD_W // SSD_GROUPS

LANES = 128
VMEM_LIMIT = 56 * 1024 * 1024

NEG_BIG = -1e30


def _pick(n, candidates):
    for c in candidates:
        if n % c == 0:
            return c
    return n


def _mm(a, b):
    return jnp.dot(a.astype(BF16), b.astype(BF16), preferred_element_type=F32)


def _mm_nt(a, b):
    return lax.dot_general(a.astype(BF16), b.astype(BF16), (((1,), (1,)), ((), ())), preferred_element_type=F32)


def _mm_tn(a, b):
    return lax.dot_general(a.astype(BF16), b.astype(BF16), (((0,), (0,)), ((), ())), preferred_element_type=F32)


def _split3(x):
    x1 = x.astype(BF16)
    r1 = x - x1.astype(F32)
    x2 = r1.astype(BF16)
    x3 = (r1 - x2.astype(F32)).astype(BF16)
    return x1, x2, x3


def _mm_sel(sel, x):
    x1, x2, x3 = _split3(x)
    d = lambda y: jnp.dot(sel, y, preferred_element_type=F32)
    return d(x1) + d(x2) + d(x3)


def _softplus(x):
    return jnp.maximum(x, 0.0) + jnp.log(1.0 + jnp.exp(-jnp.abs(x)))


def _sigmoid(x):
    return 1.0 / (1.0 + jnp.exp(-x))


def _matmul_kernel(a_ref, w_ref, o_ref):
    o_ref[...] = _mm(a_ref[...], w_ref[...]).astype(o_ref.dtype)


def _matmul(a, w, out_dtype=F32, tm_cands=(768, 512, 384, 256, 128, 8), tn_cands=(512, 256, 128), layer=None):
    m, k = a.shape
    n = w.shape[-1]
    tm = _pick(m, tm_cands)
    tn = _pick(n, tn_cands)
    if layer is None:
        w_spec = pl.BlockSpec((k, tn), lambda i, j: (0, j))
    else:
        w_spec = pl.BlockSpec((None, k, tn), lambda i, j: (layer, 0, j))
    return pl.pallas_call(
        _matmul_kernel,
        grid=(m // tm, n // tn),
        in_specs=[pl.BlockSpec((tm, k), lambda i, j: (i, 0)), w_spec],
        out_specs=pl.BlockSpec((tm, tn), lambda i, j: (i, j)),
        out_shape=jax.ShapeDtypeStruct((m, n), out_dtype),
        compiler_params=pltpu.CompilerParams(dimension_semantics=("arbitrary", "arbitrary"),
                                             vmem_limit_bytes=VMEM_LIMIT),
        name="dense_matmul",
    )(a, w)


def _merge_kernel(oa_ref, ob_ref, oc_ref, wa_ref, wb_ref, wc_ref, sa_ref, sb_ref, sc_ref, o_ref):
    dot = lambda x_ref, w_ref: jnp.dot(x_ref[...], w_ref[...], preferred_element_type=F32)
    o = (_sigmoid(sa_ref[...]) * dot(oa_ref, wa_ref) + _sigmoid(sb_ref[...]) * dot(ob_ref, wb_ref)
         + _sigmoid(sc_ref[...]) * dot(oc_ref, wc_ref))
    o_ref[...] = o.astype(o_ref.dtype)


def _merge(oa, ob, oc, wa, wb, wc, gates):
    m = oa.shape[0]
    n = wa.shape[1]
    tm = _pick(m, (768, 512, 384, 256, 128))
    tn = _pick(n, (512, 256, 128))
    nj = n // tn
    act = lambda t: pl.BlockSpec((tm, t.shape[1]), lambda i, j: (i, 0))
    wgt = lambda t: pl.BlockSpec((t.shape[0], tn), lambda i, j: (0, j))
    gate = lambda s: pl.BlockSpec((tm, tn), lambda i, j: (i, s * nj + j))
    return pl.pallas_call(
        _merge_kernel,
        grid=(m // tm, nj),
        in_specs=[act(oa), act(ob), act(oc), wgt(wa), wgt(wb), wgt(wc), gate(0), gate(1), gate(2)],
        out_specs=pl.BlockSpec((tm, tn), lambda i, j: (i, j)),
        out_shape=jax.ShapeDtypeStruct((m, n), BF16),
        compiler_params=pltpu.CompilerParams(dimension_semantics=("arbitrary", "arbitrary"),
                                             vmem_limit_bytes=VMEM_LIMIT),
        name="branch_merge",
    )(oa, ob, oc, wa, wb, wc, gates, gates, gates)


def _out_norm_kernel(n_ctx, rows_per_batch, m_ref, w_ref, x_ref, gate_ref, g_ref, o_ref, acc_ref):
    i = pl.program_id(0)
    kk = pl.program_id(1)

    @pl.when(kk == 0)
    def _():
        acc_ref[...] = jnp.zeros_like(acc_ref)

    acc_ref[...] += jnp.dot(m_ref[...], w_ref[...], preferred_element_type=F32)

    @pl.when(kk == pl.num_programs(1) - 1)
    def _():
        tm = acc_ref.shape[0]
        out = acc_ref[...]
        y = out * lax.rsqrt(jnp.mean(out * out, axis=-1, keepdims=True) + EPS) * g_ref[...]
        tiles_per_batch = rows_per_batch // tm
        bi = i // tiles_per_batch
        row = (i % tiles_per_batch) * tm + lax.broadcasted_iota(jnp.int32, (tm, 1), 0)
        nb = gate_ref.shape[0] - 1
        gate = jnp.where(row < n_ctx, gate_ref[nb:nb + 1, :], gate_ref[pl.ds(bi, 1), :])
        o_ref[...] = x_ref[...] + gate * y


def _out_norm(mrg, w, x2d, gates, g_post, n_ctx, rows_per_batch):
    m, k = mrg.shape
    n = w.shape[1]
    tm = _pick(rows_per_batch, (384, 256, 128))
    tk = _pick(k, (512, 256, 128))
    return pl.pallas_call(
        functools.partial(_out_norm_kernel, n_ctx, rows_per_batch),
        grid=(m // tm, k // tk),
        in_specs=[pl.BlockSpec((tm, tk), lambda i, kk: (i, kk)),
                  pl.BlockSpec((tk, n), lambda i, kk: (kk, 0)),
                  pl.BlockSpec((tm, n), lambda i, kk: (i, 0)),
                  pl.BlockSpec(gates.shape, lambda i, kk: (0, 0)),
                  pl.BlockSpec((1, n), lambda i, kk: (0, 0))],
        out_specs=pl.BlockSpec((tm, n), lambda i, kk: (i, 0)),
        out_shape=jax.ShapeDtypeStruct((m, n), F32),
        scratch_shapes=[pltpu.VMEM((tm, n), F32)],
        compiler_params=pltpu.CompilerParams(dimension_semantics=("arbitrary", "arbitrary"),
                                             vmem_limit_bytes=VMEM_LIMIT),
        name="out_norm",
    )(mrg, w, x2d, gates, g_post.reshape(1, n))


def _head_masks(rows):
    lane = lax.broadcasted_iota(jnp.int32, (rows, LANES), 1)
    return lane < RWKV_HD


def _bd(x, first):
    zero = jnp.zeros_like(x)
    return jnp.concatenate([jnp.where(first, x, zero), jnp.where(first, zero, x)], axis=0)


def _rwkv_local_kernel(r_ref, k_ref, v_ref, lw_ref, la_ref, w0_ref, w2_ref, a0_ref, a2_ref, kk_ref, ka_ref,
                       rk_ref, m_ref, n_ref, q_ref, yl_ref, bonus_ref):
    c = RWKV_CHUNK
    width = r_ref.shape[2]
    first = _head_masks(c)
    row = lax.broadcasted_iota(jnp.int32, (c, LANES), 0)
    col = lax.broadcasted_iota(jnp.int32, (c, LANES), 1) % RWKV_HD
    eye_cat = (row == col).astype(F32)
    hr = lax.broadcasted_iota(jnp.int32, (LANES, LANES), 0) // RWKV_HD
    hc = lax.broadcasted_iota(jnp.int32, (LANES, LANES), 1) // RWKV_HD
    head_ones = (hr == hc).astype(BF16)
    tr = lax.broadcasted_iota(jnp.int32, (c, c), 0)
    tc = lax.broadcasted_iota(jnp.int32, (c, c), 1)
    zrow = jnp.zeros((RWKV_LORA, width), F32)

    tanh_lw = jnp.tanh(lw_ref[0])
    la = la_ref[0]
    w_raw2, a_sig2 = [], []
    for d in (0, 1):
        pad = (lambda m: jnp.concatenate([m, zrow], axis=0)) if d == 0 else (lambda m: jnp.concatenate([zrow, m], axis=0))
        w_raw2.append(w0_ref[d:d + 1, :] + _mm(tanh_lw, pad(w2_ref[d])))
        a_sig2.append(_sigmoid(a0_ref[d:d + 1, :] + _mm(la, pad(a2_ref[d]))))

    r_all, k_all, v_all = r_ref[0], k_ref[0], v_ref[0]
    kk_all, ka_all, rk_all = kk_ref[...], ka_ref[...], rk_ref[...]
    consts = (first, row, col, eye_cat, tr, tc)
    chains, outs = [], {}
    for pi in range(width // LANES):
        sl = slice(pi * LANES, (pi + 1) * LANES)
        r, k, v = r_all[:, sl], k_all[:, sl], v_all[:, sl]
        kkp = k * kk_all[:, sl]
        kk = kkp / jnp.maximum(jnp.sqrt(_mm_sel_right(kkp * kkp, head_ones)), 1e-12)
        a_sum = a_sig2[0][:, sl] + a_sig2[1][:, sl]
        kd_sum = k * (2.0 + (a_sum - 2.0) * ka_all[:, sl])
        outs[pi] = _mm_sel_right(r * kd_sum * rk_all[:, sl], head_ones) * v
        for d in (0, 1):
            chains.append(_rwkv_chain(d, r, k, v, kk, w_raw2[d][:, sl], a_sig2[d][:, sl], ka_all[:, sl], consts,
                                      outs, (pi, d)))
    while chains:
        alive = []
        for ch in chains:
            if next(ch, _DONE) is not _DONE:
                alive.append(ch)
        chains = alive
    for pi in range(width // LANES):
        sl = slice(pi * LANES, (pi + 1) * LANES)
        for d in (0, 1):
            m_cat, n_cat, q, yloc = outs[(pi, d)]
            m_ref[0, d, 0, :, sl] = m_cat
            n_ref[0, d, 0, :, sl] = n_cat
            q_ref[0, d, :, sl] = q
            yl_ref[0, d, :, sl] = yloc
        bonus_ref[0, :, sl] = outs[pi]


_DONE = object()


def _rwkv_chain(d, r, k, v, kk, w_raw, a_sig, ka, consts, outs, key):
    c = RWKV_CHUNK
    first, row, col, eye_cat, tr, tc = consts
    if d == 0:
        incl, strict, tri = col <= row, col < row, (tc <= tr)
    else:
        incl, strict, tri = col >= row, col > row, (tc >= tr)
    logw = -jnp.exp(-_softplus(-w_raw) - 0.5)
    k_dir = k * (1.0 + (a_sig - 1.0) * ka)
    b_dir = kk * a_sig

    cum = _mm_sel(tri.astype(BF16), logw)
    yield
    tot = cum[c - 1:c, :] if d == 0 else cum[0:1, :]
    g_incl = jnp.exp(cum)
    g_excl = jnp.exp(cum - logw)
    g_inv = jnp.exp(-cum)
    g_hat = jnp.exp(tot - cum)
    at = -kk * g_excl
    rt = r * g_incl
    bt = b_dir * g_inv
    kt = k_dir * g_inv
    bh = b_dir * g_hat
    kh = k_dir * g_hat

    sc = _mm_nt(jnp.concatenate([at, rt], axis=0),
                jnp.concatenate([_bd(bt, first), _bd(kt, first)], axis=0))
    yield
    zero = jnp.zeros((c, LANES), F32)
    a_ab = jnp.where(strict, sc[:c, :LANES], zero)
    a_ak = jnp.where(strict, sc[:c, LANES:], zero)
    a_rb = jnp.where(incl, sc[c:, :LANES], zero)
    a_rk = jnp.where(incl, sc[c:, LANES:], zero)

    t = eye_cat + a_ab
    p = _mm(a_ab, _bd(a_ab, first))
    akv = _mm(a_ak, _bd(v, first))
    yield
    for _ in range(4):
        pbd = _bd(p, first)
        t, p = t + _mm(t, pbd), _mm(p, pbd)
        yield
    t = t + _mm(t, _bd(p, first))
    yield

    x = _mm(t, jnp.concatenate([_bd(akv, first), _bd(at, first)], axis=1))
    yield
    uloc = x[:, :LANES]
    w = x[:, LANES:]
    yloc = _mm(jnp.concatenate([a_rb, a_rk], axis=1),
               jnp.concatenate([_bd(uloc, first), _bd(v, first)], axis=0))
    q = rt + _mm(a_rb, _bd(w, first))
    g = _mm_tn(jnp.concatenate([bh, kh], axis=1),
               jnp.concatenate([uloc, w, v], axis=1))
    yield
    nfull = g[:LANES, :LANES] + g[LANES:, 2 * LANES:]
    mfull = g[:LANES, LANES:2 * LANES]
    n_cat = jnp.where(first, nfull[:c], zero) + jnp.where(first, zero, nfull[c:])
    m_cat = jnp.where(first, mfull[:c], zero) + jnp.where(first, zero, mfull[c:]) + eye_cat * jnp.exp(tot)
    outs[key] = (m_cat, n_cat, q, yloc)


def _mm_sel_right(x, sel):
    x1, x2, x3 = _split3(x)
    d = lambda y: jnp.dot(y, sel, preferred_element_type=F32)
    return d(x1) + d(x2) + d(x3)


def _rwkv_local(uconv, w0, w2, a0, a2, k_k, k_a, r_k):
    b, n_all, _ = uconv.shape
    c = RWKV_CHUNK
    nc = n_all // c
    wb = RWKV_LOCAL_WIDTH
    nblk = RWKV_W // wb
    row = lambda t: t.reshape(1, RWKV_W)
    col_blk = lambda off: pl.BlockSpec((1, c, wb), lambda bi, ci, pi: (bi, ci, off + pi))
    lora_blk = lambda off: pl.BlockSpec((1, c, LANES), lambda bi, ci, pi: (bi, ci, off))
    vec2 = pl.BlockSpec((2, wb), lambda bi, ci, pi: (0, pi))
    mat2 = pl.BlockSpec((2, RWKV_LORA, wb), lambda bi, ci, pi: (0, 0, pi))
    vec1 = pl.BlockSpec((1, wb), lambda bi, ci, pi: (0, pi))
    mn_shape = jax.ShapeDtypeStruct((b, 2, nc, RWKV_HD, RWKV_W), F32)
    mn_spec = pl.BlockSpec((1, 2, 1, RWKV_HD, wb), lambda bi, ci, pi: (bi, 0, ci, 0, pi))
    qy_shape = jax.ShapeDtypeStruct((b, 2, n_all, RWKV_W), F32)
    qy_spec = pl.BlockSpec((1, 2, c, wb), lambda bi, ci, pi: (bi, 0, ci, pi))
    lora0 = 3 * RWKV_W // LANES
    return pl.pallas_call(
        _rwkv_local_kernel,
        grid=(b, nc, nblk),
        in_specs=[col_blk(0), col_blk(nblk), col_blk(2 * nblk), lora_blk(lora0), lora_blk(lora0 + 1),
                  vec2, mat2, vec2, mat2, vec1, vec1, vec1],
        out_specs=[mn_spec, mn_spec, qy_spec, qy_spec,
                   pl.BlockSpec((1, c, wb), lambda bi, ci, pi: (bi, ci, pi))],
        out_shape=[mn_shape, mn_shape, qy_shape, qy_shape, jax.ShapeDtypeStruct((b, n_all, RWKV_W), F32)],
        compiler_params=pltpu.CompilerParams(dimension_semantics=("arbitrary",) * 3, vmem_limit_bytes=VMEM_LIMIT),
        name="rwkv_local",
    )(uconv, uconv, uconv, uconv, uconv, w0, w2, a0, a2, row(k_k), row(k_a), row(r_k))


def _rwkv_scan_kernel(m_ref, n_ref, q_ref, yl_ref, y_ref, s_ref):
    c = RWKV_CHUNK

    @pl.when(pl.program_id(2) == 0)
    def _():
        s_ref[...] = jnp.zeros_like(s_ref)

    first = _head_masks(RWKV_HD)
    for p in range(RWKV_W // LANES):
        sl = slice(p * LANES, (p + 1) * LANES)
        s1, s2, s3 = _split3(_bd(s_ref[:, sl], first))
        lhs = jnp.concatenate([q_ref[0, 0, :, sl], m_ref[0, 0, 0, :, sl]], axis=0)
        l1, l2, l3 = _split3(lhs)
        d = lambda x, y: jnp.dot(x, y, preferred_element_type=F32)
        z = d(l1, s1) + (d(l1, s2) + d(l2, s1)) + (d(l2, s2) + d(l1, s3) + d(l3, s1))
        y_ref[0, 0, :, sl] = yl_ref[0, 0, :, sl] + z[:c]
        s_ref[:, sl] = z[c:] + n_ref[0, 0, 0, :, sl]


def _chunk_order(d, i, n_ctx_chunks, n_chunks):
    back = jnp.where(i < n_ctx_chunks, n_ctx_chunks - 1 - i, n_chunks + n_ctx_chunks - 1 - i)
    return jnp.where(d == 0, i, back)


def _rwkv_scan(m, n, q, yl, n_ctx):
    b, _, nc, _, _ = m.shape
    n_all = q.shape[2]
    c = RWKV_CHUNK
    ncc = n_ctx // c
    cidx = lambda d, i: _chunk_order(d, i, ncc, nc)
    mn_spec = pl.BlockSpec((1, 1, 1, RWKV_HD, RWKV_W), lambda bi, d, i: (bi, d, cidx(d, i), 0, 0))
    qy_spec = pl.BlockSpec((1, 1, c, RWKV_W), lambda bi, d, i: (bi, d, cidx(d, i), 0))
    return pl.pallas_call(
        _rwkv_scan_kernel,
        grid=(b, 2, nc),
        in_specs=[mn_spec, mn_spec, qy_spec, qy_spec],
        out_specs=qy_spec,
        out_shape=jax.ShapeDtypeStruct((b, 2, n_all, RWKV_W), F32),
        scratch_shapes=[pltpu.VMEM((RWKV_HD, RWKV_W), F32)],
        compiler_params=pltpu.CompilerParams(dimension_semantics=("arbitrary",) * 3, vmem_limit_bytes=VMEM_LIMIT),
        name="rwkv_scan",
    )(m, n, q, yl)


ATTN_COLS = 256


def _diff_attn_kernel(tsub, lam_ref, qt_ref, k_ref, vt_ref, o_ref, qs_ref, m_ref, l_ref, acc_ref):
    tq = qt_ref.shape[2]
    tk = k_ref.shape[1]
    kv = pl.program_id(3)

    @pl.when(kv == 0)
    def _():
        qt = qt_ref[0]
        feat = lax.broadcasted_iota(jnp.int32, qt.shape, 0)
        zero = jnp.zeros_like(qt)
        qs_ref[...] = jnp.concatenate([jnp.where(feat < DIFF_HD, qt, zero), jnp.where(feat < DIFF_HD, zero, qt)], axis=1)
        m_ref[...] = jnp.full_like(m_ref, NEG_BIG)
        l_ref[...] = jnp.zeros_like(l_ref)
        acc_ref[...] = jnp.zeros_like(acc_ref)

    nsub = tk // tsub
    strips = [slice(cb * ATTN_COLS, (cb + 1) * ATTN_COLS) for cb in range(2 * tq // ATTN_COLS)]
    ones = jnp.ones((ATTN_ONES_ROWS, tsub), BF16)

    def scores_of(j):
        kj = k_ref[0, j * tsub:(j + 1) * tsub, :]
        return [jnp.dot(kj, qs_ref[:, sl], preferred_element_type=F32) for sl in strips]

    nxt = scores_of(0)
    for j in range(nsub):
        scores = nxt
        if j + 1 < nsub:
            nxt = scores_of(j + 1)
        v_ext = jnp.concatenate([vt_ref[0, :, j * tsub:(j + 1) * tsub], ones], axis=0)
        for sl, s in zip(strips, scores):
            m_old = m_ref[:, sl]
            m_new = jnp.maximum(m_old, jnp.max(s, axis=0, keepdims=True))
            alpha = jnp.exp2(m_old - m_new)
            p = jnp.exp2(s - m_new[:1]).astype(BF16)
            pv = jnp.dot(v_ext, p, preferred_element_type=F32)
            acc_ref[:, sl] = alpha[:1] * acc_ref[:, sl] + pv[:DIFF_VD]
            l_ref[:, sl] = alpha * l_ref[:, sl] + pv[DIFF_VD:DIFF_VD + 8]
            m_ref[:, sl] = m_new

    @pl.when(kv == pl.num_programs(3) - 1)
    def _():
        o = acc_ref[...] / l_ref[:1, :]
        o_ref[0] = o[:, :tq] - lam_ref[0, 0] * o[:, tq:]


ATTN_ONES_ROWS = 16
ATTN_MAX_SUBTILES = 11


def _diff_attn(qt, k, vt, lam):
    b, _, nq = qt.shape
    nk = k.shape[1]
    tq = _pick(nq, (512, 256, 128))
    tsub = _pick(nk, (768, 512, 384, 256, 128))
    n_sub = nk // tsub
    tk = tsub * max(s for s in range(1, ATTN_MAX_SUBTILES + 1) if n_sub % s == 0)
    return pl.pallas_call(
        functools.partial(_diff_attn_kernel, tsub),
        grid=(b, DIFF_HEADS, nq // tq, nk // tk),
        in_specs=[pl.BlockSpec(memory_space=pltpu.SMEM),
                  pl.BlockSpec((1, LANES, tq), lambda bi, h, i, j: (bi, h, i)),
                  pl.BlockSpec((1, tk, LANES), lambda bi, h, i, j: (bi, j, h)),
                  pl.BlockSpec((1, LANES, tk), lambda bi, h, i, j: (bi, h, j))],
        out_specs=pl.BlockSpec((1, LANES, tq), lambda bi, h, i, j: (bi, h, i)),
        out_shape=jax.ShapeDtypeStruct((b, DIFF_W, nq), F32),
        scratch_shapes=[pltpu.VMEM((LANES, 2 * tq), BF16), pltpu.VMEM((8, 2 * tq), F32),
                        pltpu.VMEM((8, 2 * tq), F32), pltpu.VMEM((DIFF_VD, 2 * tq), F32)],
        compiler_params=pltpu.CompilerParams(dimension_semantics=("arbitrary",) * 4, vmem_limit_bytes=VMEM_LIMIT),
        name="diff_attn",
    )(lam.reshape(1, 1).astype(F32), qt, k, vt)


SSD_GROUPS_PER_STEP = 4


def _ssd_scan_kernel(x_ref, b_ref, c_ref, dt_ref, bias_ref, alog_ref, y_ref, st_ref):
    t = SSD_CHUNK
    d = pl.program_id(1)
    gstep = pl.program_id(2)
    ngrp = x_ref.shape[2] // SSD_GW

    @pl.when(pl.program_id(3) == 0)
    def _():
        st_ref[...] = jnp.zeros_like(st_ref)

    dt = _softplus(dt_ref[0] + bias_ref[...])
    dta = dt * (-jnp.exp(alog_ref[...]))
    tr = lax.broadcasted_iota(jnp.int32, (t, t), 0)
    tc = lax.broadcasted_iota(jnp.int32, (t, t), 1)
    fwd = d == 0
    incl = jnp.where(fwd, tc - tr, tr - tc) <= 0
    x_all, b_all, c_all, st_all = x_ref[0], b_ref[0], c_ref[0], st_ref[...]
    outs = {}
    chains = []
    for j in range(ngrp):
        xs = slice(j * SSD_GW, (j + 1) * SSD_GW)
        ns = slice(j * SSD_STATE, (j + 1) * SSD_STATE)
        first_head = d * SSD_HEADS + (gstep * ngrp + j) * (SSD_HEADS // SSD_GROUPS)
        chains.append(_ssd_chain(x_all[:, xs], b_all[:, ns], c_all[:, ns], st_all[:, xs], dt, dta, first_head, fwd,
                                 incl, outs, j))
    while chains:
        alive = []
        for ch in chains:
            if next(ch, _DONE) is not _DONE:
                alive.append(ch)
        chains = alive
    for j in range(ngrp):
        xs = slice(j * SSD_GW, (j + 1) * SSD_GW)
        y, st_new = outs[j]
        y_ref[0, 0, :, xs] = y
        st_ref[:, xs] = st_new


def _ssd_chain(x, bm, cm, st, dt, dta, first_head, fwd, incl, outs, key):
    t = SSD_CHUNK
    e_heads = SSD_HEADS // SSD_GROUPS
    lane = lax.broadcasted_iota(jnp.int32, (t, 2 * SSD_HEADS), 1)
    lane_x = lax.broadcasted_iota(jnp.int32, (t, SSD_GW), 1) // SSD_HD
    lane_e = lax.broadcasted_iota(jnp.int32, (t, LANES), 1)
    dt_x = jnp.zeros((t, SSD_GW), F32)
    dta4 = jnp.zeros((t, LANES), F32)
    for e in range(e_heads):
        pick = lane == first_head + e
        dt_e = jnp.sum(jnp.where(pick, dt, 0.0), axis=1, keepdims=True)
        dta_e = jnp.sum(jnp.where(pick, dta, 0.0), axis=1, keepdims=True)
        dt_x = jnp.where(lane_x == e, dt_e, dt_x)
        dta4 = jnp.where(lane_e == e, dta_e, dta4)

    cum4 = _mm_sel(incl.astype(BF16), dta4)
    cb = _mm_nt(cm, bm)
    y_off_raw = _mm(cm, st)
    yield
    cum_t = cum4.T
    last4 = jnp.where(fwd, cum4[t - 1:t, :], cum4[0:1, :])
    cum_x = jnp.zeros((t, SSD_GW), F32)
    last_x = jnp.zeros((1, SSD_GW), F32)
    for e in range(e_heads):
        cum_x = jnp.where(lane_x == e, cum4[:, e:e + 1], cum_x)
        last_x = jnp.where(lane_x[:1] == e, last4[:, e:e + 1], last_x)

    xdt = x * dt_x
    zero_h = jnp.zeros((t, LANES), F32)
    lane_h = lax.broadcasted_iota(jnp.int32, (t, LANES), 1) < SSD_HD
    ydiag = []
    for pair in range(e_heads // 2):
        lmats = []
        for e in (2 * pair, 2 * pair + 1):
            seg = cum4[:, e:e + 1] - cum_t[e:e + 1, :]
            lmats.append(cb * jnp.exp(jnp.where(incl, seg, NEG_BIG)))
        xp = xdt[:, pair * LANES:(pair + 1) * LANES]
        xbd = jnp.concatenate([jnp.where(lane_h, xp, zero_h), jnp.where(lane_h, zero_h, xp)], axis=0)
        ydiag.append(_mm(jnp.concatenate(lmats, axis=1), xbd))
    states = _mm_tn(bm, xdt * jnp.exp(last_x - cum_x))
    yield
    y = jnp.concatenate(ydiag, axis=1) + y_off_raw * jnp.exp(cum_x)
    outs[key] = (y, st * jnp.exp(last_x) + states)


def _ssd_scan(xbc, dt_raw, dt_bias, a_log, n_ctx):
    b, n_all, _ = xbc.shape
    t = SSD_CHUNK
    nc = n_all // t
    ncc = n_ctx // t
    cidx = lambda d, i: _chunk_order(d, i, ncc, nc)
    gg = SSD_GROUPS_PER_STEP
    nsteps_g = SSD_GROUPS // gg
    xw, nw = gg * SSD_GW, gg * SSD_STATE
    b0 = SSD_W // nw
    row = pl.BlockSpec((1, 2 * SSD_HEADS), lambda bi, d, g, i: (0, 0))
    return pl.pallas_call(
        _ssd_scan_kernel,
        grid=(b, 2, nsteps_g, nc),
        in_specs=[pl.BlockSpec((1, t, xw), lambda bi, d, g, i: (bi, cidx(d, i), g)),
                  pl.BlockSpec((1, t, nw), lambda bi, d, g, i: (bi, cidx(d, i), b0 + g)),
                  pl.BlockSpec((1, t, nw), lambda bi, d, g, i: (bi, cidx(d, i), b0 + nsteps_g + g)),
                  pl.BlockSpec((1, t, 2 * SSD_HEADS), lambda bi, d, g, i: (bi, cidx(d, i), 0)),
                  row, row],
        out_specs=pl.BlockSpec((1, 1, t, xw), lambda bi, d, g, i: (bi, d, cidx(d, i), g)),
        out_shape=jax.ShapeDtypeStruct((b, 2, n_all, SSD_W), F32),
        scratch_shapes=[pltpu.VMEM((SSD_STATE, xw), F32)],
        compiler_params=pltpu.CompilerParams(dimension_semantics=("arbitrary",) * 4, vmem_limit_bytes=VMEM_LIMIT),
        name="ssd_scan",
    )(xbc, xbc, xbc, dt_raw, dt_bias.reshape(1, -1), a_log.reshape(1, -1))


def _rms_norm(x, g, eps=EPS):
    return x * lax.rsqrt(jnp.mean(x * x, axis=-1, keepdims=True) + eps) * g


def _dwconv(u, w):
    width = w.shape[0]
    pad = width // 2
    n = u.shape[1]
    up = jnp.pad(u, ((0, 0), (pad, pad), (0, 0)))
    out = up[:, 0:n] * w[0]
    for i in range(1, width):
        out = out + up[:, i:i + n] * w[i]
    return out


def _dwconv_split(u, w, n_ctx):
    return jnp.concatenate([_dwconv(u[:, :n_ctx], w), _dwconv(u[:, n_ctx:], w)], axis=1)


def _axial_rope(n_tok):
    rows = n_tok // GRID_W
    row = jnp.repeat(jnp.arange(rows, dtype=F32), GRID_W)
    col = jnp.tile(jnp.arange(GRID_W, dtype=F32), rows)
    n_freq = DIFF_HD // 4
    inv = ROPE_BASE ** (-jnp.arange(n_freq, dtype=F32) / n_freq)
    ang = jnp.concatenate([row[:, None] * inv, col[:, None] * inv], axis=-1)
    return jnp.cos(ang), jnp.sin(ang)


def _apply_rope(t, cos, sin):
    b, n, _ = t.shape
    t5 = t.reshape(b, n, DIFF_HEADS, 2, DIFF_HD)
    half = DIFF_HD // 2
    c = cos[None, :, None, None, :]
    s = sin[None, :, None, None, :]
    t1, t2 = t5[..., :half], t5[..., half:]
    return jnp.concatenate([t1 * c - t2 * s, t2 * c + t1 * s], axis=-1).reshape(b, n, DIFF_W)


def _rwkv_branch(urw, g, n_ctx, conv_w, w0, w2, a0, a2, k_k, k_a, r_k, lnx_w, lnx_b):
    b, n_all, _ = urw.shape
    uconv = _dwconv_split(urw, conv_w, n_ctx)
    m, n, q, yl, bonus = _rwkv_local(uconv, w0, w2, a0, a2, k_k, k_a, r_k)
    y2 = _rwkv_scan(m, n, q, yl, n_ctx)
    y = (y2[:, 0] + y2[:, 1]).reshape(b, n_all, RWKV_HEADS, RWKV_HD)
    mu = jnp.mean(y, axis=-1, keepdims=True)
    var = jnp.mean(jnp.square(y - mu), axis=-1, keepdims=True)
    y = ((y - mu) * lax.rsqrt(var + RWKV_GN_EPS)).reshape(b, n_all, RWKV_W) * lnx_w + lnx_b
    return (y + bonus) * jax.nn.silu(g)


def _diff_branch(q, k, v, g, n_ctx, cos, sin, lam_p, subln_g, lam_init):
    b, n_all, _ = q.shape
    lam = jnp.exp(jnp.sum(lam_p[0] * lam_p[1])) - jnp.exp(jnp.sum(lam_p[2] * lam_p[3])) + lam_init
    scale = DIFF_HD ** -0.5 * math.log2(math.e)
    qc, ql = q[:, :n_ctx], _apply_rope(q[:, n_ctx:], cos, sin)
    kc, kl = k[:, :n_ctx], _apply_rope(k[:, n_ctx:], cos, sin)
    k_all = jnp.concatenate([kc, kl], axis=1).astype(BF16)
    vt_all = jnp.swapaxes(v, 1, 2).astype(BF16)
    feat_major = lambda t: jnp.swapaxes(t * scale, 1, 2).astype(BF16)
    oc = _diff_attn(feat_major(qc), k_all[:, :n_ctx], vt_all[:, :, :n_ctx], lam)
    ol = _diff_attn(feat_major(ql), k_all, vt_all, lam)
    o = jnp.swapaxes(jnp.concatenate([oc, ol], axis=2), 1, 2).reshape(b, n_all, DIFF_HEADS, DIFF_VD)
    o = _rms_norm(o, subln_g) * (1.0 - lam_init)
    return o.reshape(b, n_all, DIFF_W) * jax.nn.silu(g)


def _ssd_branch(z, xbc, dt_raw, n_ctx, conv_w, conv_b, a_log, dt_bias, d_skip, norm_g):
    b, n_all, _ = z.shape
    xa = jax.nn.silu(_dwconv_split(xbc, conv_w, n_ctx) + conv_b)
    y2 = _ssd_scan(xa, dt_raw, dt_bias, a_log, n_ctx)
    xs = xa[..., :SSD_W]
    y = y2[:, 0] + y2[:, 1] + jnp.repeat(d_skip, SSD_HD) * xs
    y = y * jax.nn.silu(z)
    yg = y.reshape(b, n_all, SSD_GROUPS, SSD_GW)
    yg = yg * lax.rsqrt(jnp.mean(yg * yg, axis=-1, keepdims=True) + EPS)
    return yg.reshape(b, n_all, SSD_W) * norm_g


def kernel(x, c, ctx, c_ctx, w_mod, b_mod, g_pre, g_post, w_in, rwkv_conv, rwkv_w0, rwkv_w2, rwkv_a0, rwkv_a2,
           rwkv_k_k, rwkv_k_a, rwkv_r_k, rwkv_lnx_w, rwkv_lnx_b, diff_lambda, diff_subln, ssd_conv_w, ssd_conv_b,
           ssd_a_log, ssd_dt_bias, ssd_d, ssd_norm, w_branch_a, w_branch_b, w_branch_c, w_out):
    b, n_lat, dm = x.shape
    n_ctx = ctx.shape[1]
    n_all = n_ctx + n_lat
    depth = w_in.shape[0]
    cos, sin = _axial_rope(n_lat)
    cond = jax.nn.silu(jnp.concatenate([c, c_ctx[None, :]], axis=0))
    cond = jnp.pad(cond, ((0, 8 - (b + 1)), (0, 0)))

    o_rw, o_g, o_dt = 0, RWKV_CONV_CH, RWKV_CONV_CH + RWKV_W
    src_dt = o_dt + 4 * DIFF_W + SSD_W + SSD_XBC
    pad_a = -(o_dt + 2 * SSD_HEADS) % 512

    xa = jnp.concatenate([ctx, x], axis=1)
    for li in range(depth):
        lam_init = 0.8 - 0.6 * math.exp(-0.3 * li)
        mod = _matmul(cond, w_mod, tn_cands=(1024, 512, 256, 128), layer=li)[:b + 1] + b_mod[li]
        shift_l, scale_l, gate_l = jnp.split(mod[:b], 3, axis=-1)
        shift_c, scale_c, gate_c = jnp.split(mod[b], 3, axis=-1)
        is_ctx = (jnp.arange(n_all) < n_ctx)[None, :, None]
        sel = lambda vc, vl: jnp.where(is_ctx, vc[None, None, :], vl[:, None, :])
        h = _rms_norm(xa, g_pre[li]) * (1.0 + sel(scale_c, scale_l)) + sel(shift_c, shift_l)
        wl = w_in[li]
        hb = h.astype(BF16).reshape(b * n_all, dm)
        proj = lambda w: _matmul(hb, w.astype(BF16)).reshape(b, n_all, -1)
        ua = proj(jnp.concatenate([wl[:, :o_dt], wl[:, src_dt:src_dt + 2 * SSD_HEADS],
                                   jnp.zeros((dm, pad_a), wl.dtype)], axis=1))
        ub = proj(wl[:, o_dt:o_dt + 4 * DIFF_W])
        uc = proj(wl[:, o_dt + 4 * DIFF_W:src_dt])
        ug = proj(wl[:, src_dt + 2 * SSD_HEADS:])

        oa = _rwkv_branch(ua[..., o_rw:o_g], ua[..., o_g:o_dt], n_ctx, rwkv_conv[li], rwkv_w0[li], rwkv_w2[li],
                          rwkv_a0[li], rwkv_a2[li], rwkv_k_k[li], rwkv_k_a[li], rwkv_r_k[li], rwkv_lnx_w[li],
                          rwkv_lnx_b[li])
        ob = _diff_branch(ub[..., :DIFF_W], ub[..., DIFF_W:2 * DIFF_W], ub[..., 2 * DIFF_W:3 * DIFF_W],
                          ub[..., 3 * DIFF_W:], n_ctx, cos, sin, diff_lambda[li], diff_subln[li], lam_init)
        oc = _ssd_branch(uc[..., :SSD_W], uc[..., SSD_W:], ua[..., o_dt:o_dt + 2 * SSD_HEADS], n_ctx,
                         ssd_conv_w[li], ssd_conv_b[li], ssd_a_log[li], ssd_dt_bias[li], ssd_d[li], ssd_norm[li])

        flat = lambda t: t.astype(BF16).reshape(b * n_all, -1)
        mrg = _merge(flat(oa), flat(ob), flat(oc), w_branch_a[li].astype(BF16), w_branch_b[li].astype(BF16),
                     w_branch_c[li].astype(BF16), ug.reshape(b * n_all, 3 * dm))
        gates = jnp.concatenate([gate_l, gate_c[None, :]], axis=0)
        xa = _out_norm(mrg, w_out[li].astype(BF16), xa.reshape(b * n_all, dm), gates, g_post[li], n_ctx,
                       n_all).reshape(b, n_all, dm)
    return xa[:, n_ctx:]
```

```python
import functools
import math

import jax
import jax.numpy as jnp
from jax import lax
from jax.experimental import pallas as pl
from jax.experimental.pallas import tpu as pltpu

F32 = jnp.float32
BF16 = jnp.bfloat16

EPS = 1e-6
GRID_W = 64
ROPE_BASE = 10000.0

RWKV_HEADS = 16
RWKV_HD = 64
RWKV_W = RWKV_HEADS * RWKV_HD
RWKV_LORA = 64
RWKV_CONV_CH = 3 * RWKV_W + 4 * RWKV_LORA
RWKV_GN_EPS = 64e-5
RWKV_CHUNK = 64
RWKV_LOCAL_WIDTH = 1024

DIFF_HEADS = 8
DIFF_HD = 64
DIFF_VD = 2 * DIFF_HD
DIFF_W = DIFF_HEADS * DIFF_VD

SSD_HEADS = 32
SSD_HD = 64
SSD_W = SSD_HEADS * SSD_HD
SSD_GROUPS = 8
SSD_STATE = 128
SSD_CHUNK = 128
SSD_XBC = SSD_W + 2 * SSD_GROUPS * SSD_STATE
SSD_GW = SSD_W // SSD_GROUPS

LANES = 128
VMEM_LIMIT = 56 * 1024 * 1024

NEG_BIG = -1e30


def _pick(n, candidates):
    for c in candidates:
        if n % c == 0:
            return c
    return n


def _mm(a, b):
    return jnp.dot(a.astype(BF16), b.astype(BF16), preferred_element_type=F32)


def _mm_nt(a, b):
    return lax.dot_general(a.astype(BF16), b.astype(BF16), (((1,), (1,)), ((), ())), preferred_element_type=F32)


def _mm_tn(a, b):
    return lax.dot_general(a.astype(BF16), b.astype(BF16), (((0,), (0,)), ((), ())), preferred_element_type=F32)


def _split3(x):
    x1 = x.astype(BF16)
    r1 = x - x1.astype(F32)
    x2 = r1.astype(BF16)
    x3 = (r1 - x2.astype(F32)).astype(BF16)
    return x1, x2, x3


def _mm_sel(sel, x):
    x1, x2, x3 = _split3(x)
    d = lambda y: jnp.dot(sel, y, preferred_element_type=F32)
    return d(x1) + d(x2) + d(x3)


def _softplus(x):
    return jnp.maximum(x, 0.0) + jnp.log(1.0 + jnp.exp(-jnp.abs(x)))


def _sigmoid(x):
    return 1.0 / (1.0 + jnp.exp(-x))


def _matmul_kernel(a_ref, w_ref, o_ref):
    o_ref[...] = _mm(a_ref[...], w_ref[...]).astype(o_ref.dtype)


def _matmul(a, w, out_dtype=F32, tm_cands=(768, 512, 384, 256, 128, 8), tn_cands=(512, 256, 128), layer=None):
    m, k = a.shape
    n = w.shape[-1]
    tm = _pick(m, tm_cands)
    tn = _pick(n, tn_cands)
    if layer is None:
        w_spec = pl.BlockSpec((k, tn), lambda i, j: (0, j))
    else:
        w_spec = pl.BlockSpec((None, k, tn), lambda i, j: (layer, 0, j))
    return pl.pallas_call(
        _matmul_kernel,
        grid=(m // tm, n // tn),
        in_specs=[pl.BlockSpec((tm, k), lambda i, j: (i, 0)), w_spec],
        out_specs=pl.BlockSpec((tm, tn), lambda i, j: (i, j)),
        out_shape=jax.ShapeDtypeStruct((m, n), out_dtype),
        compiler_params=pltpu.CompilerParams(dimension_semantics=("arbitrary", "arbitrary"),
                                             vmem_limit_bytes=VMEM_LIMIT),
        name="dense_matmul",
    )(a, w)


def _merge_kernel(oa_ref, ob_ref, oc_ref, wa_ref, wb_ref, wc_ref, sa_ref, sb_ref, sc_ref, o_ref):
    dot = lambda x_ref, w_ref: jnp.dot(x_ref[...], w_ref[...], preferred_element_type=F32)
    o = (_sigmoid(sa_ref[...]) * dot(oa_ref, wa_ref) + _sigmoid(sb_ref[...]) * dot(ob_ref, wb_ref)
         + _sigmoid(sc_ref[...]) * dot(oc_ref, wc_ref))
    o_ref[...] = o.astype(o_ref.dtype)


def _merge(oa, ob, oc, wa, wb, wc, gates):
    m = oa.shape[0]
    n = wa.shape[1]
    tm = _pick(m, (768, 512, 384, 256, 128))
    tn = _pick(n, (512, 256, 128))
    nj = n // tn
    act = lambda t: pl.BlockSpec((tm, t.shape[1]), lambda i, j: (i, 0))
    wgt = lambda t: pl.BlockSpec((t.shape[0], tn), lambda i, j: (0, j))
    gate = lambda s: pl.BlockSpec((tm, tn), lambda i, j: (i, s * nj + j))
    return pl.pallas_call(
        _merge_kernel,
        grid=(m // tm, nj),
        in_specs=[act(oa), act(ob), act(oc), wgt(wa), wgt(wb), wgt(wc), gate(0), gate(1), gate(2)],
        out_specs=pl.BlockSpec((tm, tn), lambda i, j: (i, j)),
        out_shape=jax.ShapeDtypeStruct((m, n), BF16),
        compiler_params=pltpu.CompilerParams(dimension_semantics=("arbitrary", "arbitrary"),
                                             vmem_limit_bytes=VMEM_LIMIT),
        name="branch_merge",
    )(oa, ob, oc, wa, wb, wc, gates, gates, gates)


def _out_norm_kernel(n_ctx, rows_per_batch, m_ref, w_ref, x_ref, gate_ref, g_ref, o_ref, acc_ref):
    i = pl.program_id(0)
    kk = pl.program_id(1)

    @pl.when(kk == 0)
    def _():
        acc_ref[...] = jnp.zeros_like(acc_ref)

    acc_ref[...] += jnp.dot(m_ref[...], w_ref[...], preferred_element_type=F32)

    @pl.when(kk == pl.num_programs(1) - 1)
    def _():
        tm = acc_ref.shape[0]
        out = acc_ref[...]
        y = out * lax.rsqrt(jnp.mean(out * out, axis=-1, keepdims=True) + EPS) * g_ref[...]
        tiles_per_batch = rows_per_batch // tm
        bi = i // tiles_per_batch
        row = (i % tiles_per_batch) * tm + lax.broadcasted_iota(jnp.int32, (tm, 1), 0)
        nb = gate_ref.shape[0] - 1
        gate = jnp.where(row < n_ctx, gate_ref[nb:nb + 1, :], gate_ref[pl.ds(bi, 1), :])
        o_ref[...] = x_ref[...] + gate * y


def _out_norm(mrg, w, x2d, gates, g_post, n_ctx, rows_per_batch):
    m, k = mrg.shape
    n = w.shape[1]
    tm = _pick(rows_per_batch, (384, 256, 128))
    tk = _pick(k, (512, 256, 128))
    return pl.pallas_call(
        functools.partial(_out_norm_kernel, n_ctx, rows_per_batch),
        grid=(m // tm, k // tk),
        in_specs=[pl.BlockSpec((tm, tk), lambda i, kk: (i, kk)),
                  pl.BlockSpec((tk, n), lambda i, kk: (kk, 0)),
                  pl.BlockSpec((tm, n), lambda i, kk: (i, 0)),
                  pl.BlockSpec(gates.shape, lambda i, kk: (0, 0)),
                  pl.BlockSpec((1, n), lambda i, kk: (0, 0))],
        out_specs=pl.BlockSpec((tm, n), lambda i, kk: (i, 0)),
        out_shape=jax.ShapeDtypeStruct((m, n), F32),
        scratch_shapes=[pltpu.VMEM((tm, n), F32)],
        compiler_params=pltpu.CompilerParams(dimension_semantics=("arbitrary", "arbitrary"),
                                             vmem_limit_bytes=VMEM_LIMIT),
        name="out_norm",
    )(mrg, w, x2d, gates, g_post.reshape(1, n))


def _head_masks(rows):
    lane = lax.broadcasted_iota(jnp.int32, (rows, LANES), 1)
    return lane < RWKV_HD


def _bd(x, first):
    zero = jnp.zeros_like(x)
    return jnp.concatenate([jnp.where(first, x, zero), jnp.where(first, zero, x)], axis=0)


def _rwkv_local_kernel(r_ref, k_ref, v_ref, lw_ref, la_ref, w0_ref, w2_ref, a0_ref, a2_ref, kk_ref, ka_ref,
                       rk_ref, m_ref, n_ref, q_ref, yl_ref, bonus_ref):
    c = RWKV_CHUNK
    width = r_ref.shape[2]
    first = _head_masks(c)
    row = lax.broadcasted_iota(jnp.int32, (c, LANES), 0)
    col = lax.broadcasted_iota(jnp.int32, (c, LANES), 1) % RWKV_HD
    eye_cat = (row == col).astype(F32)
    hr = lax.broadcasted_iota(jnp.int32, (LANES, LANES), 0) // RWKV_HD
    hc = lax.broadcasted_iota(jnp.int32, (LANES, LANES), 1) // RWKV_HD
    head_ones = (hr == hc).astype(BF16)
    tr = lax.broadcasted_iota(jnp.int32, (c, c), 0)
    tc = lax.broadcasted_iota(jnp.int32, (c, c), 1)
    zrow = jnp.zeros((RWKV_LORA, width), F32)

    tanh_lw = jnp.tanh(lw_ref[0])
    la = la_ref[0]
    w_raw2, a_sig2 = [], []
    for d in (0, 1):
        pad = (lambda m: jnp.concatenate([m, zrow], axis=0)) if d == 0 else (lambda m: jnp.concatenate([zrow, m], axis=0))
        w_raw2.append(w0_ref[d:d + 1, :] + _mm(tanh_lw, pad(w2_ref[d])))
        a_sig2.append(_sigmoid(a0_ref[d:d + 1, :] + _mm(la, pad(a2_ref[d]))))

    r_all, k_all, v_all = r_ref[0], k_ref[0], v_ref[0]
    kk_all, ka_all, rk_all = kk_ref[...], ka_ref[...], rk_ref[...]
    consts = (first, row, col, eye_cat, tr, tc)
    chains, outs = [], {}
    for pi in range(width // LANES):
        sl = slice(pi * LANES, (pi + 1) * LANES)
        r, k, v = r_all[:, sl], k_all[:, sl], v_all[:, sl]
        kkp = k * kk_all[:, sl]
        kk = kkp / jnp.maximum(jnp.sqrt(_mm_sel_right(kkp * kkp, head_ones)), 1e-12)
        a_sum = a_sig2[0][:, sl] + a_sig2[1][:, sl]
        kd_sum = k * (2.0 + (a_sum - 2.0) * ka_all[:, sl])
        outs[pi] = _mm_sel_right(r * kd_sum * rk_all[:, sl], head_ones) * v
        for d in (0, 1):
            chains.append(_rwkv_chain(d, r, k, v, kk, w_raw2[d][:, sl], a_sig2[d][:, sl], ka_all[:, sl], consts,
                                      outs, (pi, d)))
    while chains:
        alive = []
        for ch in chains:
            if next(ch, _DONE) is not _DONE:
                alive.append(ch)
        chains = alive
    for pi in range(width // LANES):
        sl = slice(pi * LANES, (pi + 1) * LANES)
        for d in (0, 1):
            m_cat, n_cat, q, yloc = outs[(pi, d)]
            m_ref[0, d, 0, :, sl] = m_cat
            n_ref[0, d, 0, :, sl] = n_cat
            q_ref[0, d, :, sl] = q
            yl_ref[0, d, :, sl] = yloc
        bonus_ref[0, :, sl] = outs[pi]


_DONE = object()


def _rwkv_chain(d, r, k, v, kk, w_raw, a_sig, ka, consts, outs, key):
    c = RWKV_CHUNK
    first, row, col, eye_cat, tr, tc = consts
    if d == 0:
        incl, strict, tri = col <= row, col < row, (tc <= tr)
    else:
        incl, strict, tri = col >= row, col > row, (tc >= tr)
    logw = -jnp.exp(-_softplus(-w_raw) - 0.5)
    k_dir = k * (1.0 + (a_sig - 1.0) * ka)
    b_dir = kk * a_sig

    cum = _mm_sel(tri.astype(BF16), logw)
    yield
    tot = cum[c - 1:c, :] if d == 0 else cum[0:1, :]
    g_incl = jnp.exp(cum)
    g_excl = jnp.exp(cum - logw)
    g_inv = jnp.exp(-cum)
    g_hat = jnp.exp(tot - cum)
    at = -kk * g_excl
    rt = r * g_incl
    bt = b_dir * g_inv
    kt = k_dir * g_inv
    bh = b_dir * g_hat
    kh = k_dir * g_hat

    sc = _mm_nt(jnp.concatenate([at, rt], axis=0),
                jnp.concatenate([_bd(bt, first), _bd(kt, first)], axis=0))
    yield
    zero = jnp.zeros((c, LANES), F32)
    a_ab = jnp.where(strict, sc[:c, :LANES], zero)
    a_ak = jnp.where(strict, sc[:c, LANES:], zero)
    a_rb = jnp.where(incl, sc[c:, :LANES], zero)
    a_rk = jnp.where(incl, sc[c:, LANES:], zero)

    t = eye_cat + a_ab
    p = _mm(a_ab, _bd(a_ab, first))
    akv = _mm(a_ak, _bd(v, first))
    yield
    for _ in range(4):
        pbd = _bd(p, first)
        t, p = t + _mm(t, pbd), _mm(p, pbd)
        yield
    t = t + _mm(t, _bd(p, first))
    yield

    x = _mm(t, jnp.concatenate([_bd(akv, first), _bd(at, first)], axis=1))
    yield
    uloc = x[:, :LANES]
    w = x[:, LANES:]
    yloc = _mm(jnp.concatenate([a_rb, a_rk], axis=1),
               jnp.concatenate([_bd(uloc, first), _bd(v, first)], axis=0))
    q = rt + _mm(a_rb, _bd(w, first))
    g = _mm_tn(jnp.concatenate([bh, kh], axis=1),
               jnp.concatenate([uloc, w, v], axis=1))
    yield
    nfull = g[:LANES, :LANES] + g[LANES:, 2 * LANES:]
    mfull = g[:LANES, LANES:2 * LANES]
    n_cat = jnp.where(first, nfull[:c], zero) + jnp.where(first, zero, nfull[c:])
    m_cat = jnp.where(first, mfull[:c], zero) + jnp.where(first, zero, mfull[c:]) + eye_cat * jnp.exp(tot)
    outs[key] = (m_cat, n_cat, q, yloc)


def _mm_sel_right(x, sel):
    x1, x2, x3 = _split3(x)
    d = lambda y: jnp.dot(y, sel, preferred_element_type=F32)
    return d(x1) + d(x2) + d(x3)


def _rwkv_local(uconv, w0, w2, a0, a2, k_k, k_a, r_k):
    b, n_all, _ = uconv.shape
    c = RWKV_CHUNK
    nc = n_all // c
    wb = RWKV_LOCAL_WIDTH
    nblk = RWKV_W // wb
    row = lambda t: t.reshape(1, RWKV_W)
    col_blk = lambda off: pl.BlockSpec((1, c, wb), lambda bi, ci, pi: (bi, ci, off + pi))
    lora_blk = lambda off: pl.BlockSpec((1, c, LANES), lambda bi, ci, pi: (bi, ci, off))
    vec2 = pl.BlockSpec((2, wb), lambda bi, ci, pi: (0, pi))
    mat2 = pl.BlockSpec((2, RWKV_LORA, wb), lambda bi, ci, pi: (0, 0, pi))
    vec1 = pl.BlockSpec((1, wb), lambda bi, ci, pi: (0, pi))
    mn_shape = jax.ShapeDtypeStruct((b, 2, nc, RWKV_HD, RWKV_W), F32)
    mn_spec = pl.BlockSpec((1, 2, 1, RWKV_HD, wb), lambda bi, ci, pi: (bi, 0, ci, 0, pi))
    qy_shape = jax.ShapeDtypeStruct((b, 2, n_all, RWKV_W), F32)
    qy_spec = pl.BlockSpec((1, 2, c, wb), lambda bi, ci, pi: (bi, 0, ci, pi))
    lora0 = 3 * RWKV_W // LANES
    return pl.pallas_call(
        _rwkv_local_kernel,
        grid=(b, nc, nblk),
        in_specs=[col_blk(0), col_blk(nblk), col_blk(2 * nblk), lora_blk(lora0), lora_blk(lora0 + 1),
                  vec2, mat2, vec2, mat2, vec1, vec1, vec1],
        out_specs=[mn_spec, mn_spec, qy_spec, qy_spec,
                   pl.BlockSpec((1, c, wb), lambda bi, ci, pi: (bi, ci, pi))],
        out_shape=[mn_shape, mn_shape, qy_shape, qy_shape, jax.ShapeDtypeStruct((b, n_all, RWKV_W), F32)],
        compiler_params=pltpu.CompilerParams(dimension_semantics=("arbitrary",) * 3, vmem_limit_bytes=VMEM_LIMIT),
        name="rwkv_local",
    )(uconv, uconv, uconv, uconv, uconv, w0, w2, a0, a2, row(k_k), row(k_a), row(r_k))


def _rwkv_scan_kernel(m_ref, n_ref, q_ref, yl_ref, y_ref, s_ref):
    c = RWKV_CHUNK

    @pl.when(pl.program_id(2) == 0)
    def _():
        s_ref[...] = jnp.zeros_like(s_ref)

    first = _head_masks(RWKV_HD)
    for p in range(RWKV_W // LANES):
        sl = slice(p * LANES, (p + 1) * LANES)
        s1, s2, s3 = _split3(_bd(s_ref[:, sl], first))
        lhs = jnp.concatenate([q_ref[0, 0, :, sl], m_ref[0, 0, 0, :, sl]], axis=0)
        l1, l2, l3 = _split3(lhs)
        d = lambda x, y: jnp.dot(x, y, preferred_element_type=F32)
        z = d(l1, s1) + (d(l1, s2) + d(l2, s1)) + (d(l2, s2) + d(l1, s3) + d(l3, s1))
        y_ref[0, 0, :, sl] = yl_ref[0, 0, :, sl] + z[:c]
        s_ref[:, sl] = z[c:] + n_ref[0, 0, 0, :, sl]


def _chunk_order(d, i, n_ctx_chunks, n_chunks):
    back = jnp.where(i < n_ctx_chunks, n_ctx_chunks - 1 - i, n_chunks + n_ctx_chunks - 1 - i)
    return jnp.where(d == 0, i, back)


def _rwkv_scan(m, n, q, yl, n_ctx):
    b, _, nc, _, _ = m.shape
    n_all = q.shape[2]
    c = RWKV_CHUNK
    ncc = n_ctx // c
    cidx = lambda d, i: _chunk_order(d, i, ncc, nc)
    mn_spec = pl.BlockSpec((1, 1, 1, RWKV_HD, RWKV_W), lambda bi, d, i: (bi, d, cidx(d, i), 0, 0))
    qy_spec = pl.BlockSpec((1, 1, c, RWKV_W), lambda bi, d, i: (bi, d, cidx(d, i), 0))
    return pl.pallas_call(
        _rwkv_scan_kernel,
        grid=(b, 2, nc),
        in_specs=[mn_spec, mn_spec, qy_spec, qy_spec],
        out_specs=qy_spec,
        out_shape=jax.ShapeDtypeStruct((b, 2, n_all, RWKV_W), F32),
        scratch_shapes=[pltpu.VMEM((RWKV_HD, RWKV_W), F32)],
        compiler_params=pltpu.CompilerParams(dimension_semantics=("arbitrary",) * 3, vmem_limit_bytes=VMEM_LIMIT),
        name="rwkv_scan",
    )(m, n, q, yl)


ATTN_COLS = 256


def _diff_attn_kernel(tsub, lam_ref, qt_ref, k_ref, vt_ref, o_ref, qs_ref, m_ref, l_ref, acc_ref):
    tq = qt_ref.shape[2]
    tk = k_ref.shape[1]
    kv = pl.program_id(3)

    @pl.when(kv == 0)
    def _():
        qt = qt_ref[0]
        feat = lax.broadcasted_iota(jnp.int32, qt.shape, 0)
        zero = jnp.zeros_like(qt)
        qs_ref[...] = jnp.concatenate([jnp.where(feat < DIFF_HD, qt, zero), jnp.where(feat < DIFF_HD, zero, qt)], axis=1)
        m_ref[...] = jnp.full_like(m_ref, NEG_BIG)
        l_ref[...] = jnp.zeros_like(l_ref)
        acc_ref[...] = jnp.zeros_like(acc_ref)

    nsub = tk // tsub
    strips = [slice(cb * ATTN_COLS, (cb + 1) * ATTN_COLS) for cb in range(2 * tq // ATTN_COLS)]
    ones = jnp.ones((ATTN_ONES_ROWS, tsub), BF16)

    def scores_of(j):
        kj = k_ref[0, j * tsub:(j + 1) * tsub, :]
        return [jnp.dot(kj, qs_ref[:, sl], preferred_element_type=F32) for sl in strips]

    nxt = scores_of(0)
    for j in range(nsub):
        scores = nxt
        if j + 1 < nsub:
            nxt = scores_of(j + 1)
        v_ext = jnp.concatenate([vt_ref[0, :, j * tsub:(j + 1) * tsub], ones], axis=0)
        for sl, s in zip(strips, scores):
            m_old = m_ref[:, sl]
            m_new = jnp.maximum(m_old, jnp.max(s, axis=0, keepdims=True))
            alpha = jnp.exp2(m_old - m_new)
            p = jnp.exp2(s - m_new[:1]).astype(BF16)
            pv = jnp.dot(v_ext, p, preferred_element_type=F32)
            acc_ref[:, sl] = alpha[:1] * acc_ref[:, sl] + pv[:DIFF_VD]
            l_ref[:, sl] = alpha * l_ref[:, sl] + pv[DIFF_VD:DIFF_VD + 8]
            m_ref[:, sl] = m_new

    @pl.when(kv == pl.num_programs(3) - 1)
    def _():
        o = acc_ref[...] / l_ref[:1, :]
        o_ref[0] = o[:, :tq] - lam_ref[0, 0] * o[:, tq:]


ATTN_ONES_ROWS = 16
ATTN_MAX_SUBTILES = 11


def _diff_attn(qt, k, vt, lam):
    b, _, nq = qt.shape
    nk = k.shape[1]
    tq = _pick(nq, (512, 256, 128))
    tsub = _pick(nk, (768, 512, 384, 256, 128))
    n_sub = nk // tsub
    tk = tsub * max(s for s in range(1, ATTN_MAX_SUBTILES + 1) if n_sub % s == 0)
    return pl.pallas_call(
        functools.partial(_diff_attn_kernel, tsub),
        grid=(b, DIFF_HEADS, nq // tq, nk // tk),
        in_specs=[pl.BlockSpec(memory_space=pltpu.SMEM),
                  pl.BlockSpec((1, LANES, tq), lambda bi, h, i, j: (bi, h, i)),
                  pl.BlockSpec((1, tk, LANES), lambda bi, h, i, j: (bi, j, h)),
                  pl.BlockSpec((1, LANES, tk), lambda bi, h, i, j: (bi, h, j))],
        out_specs=pl.BlockSpec((1, LANES, tq), lambda bi, h, i, j: (bi, h, i)),
        out_shape=jax.ShapeDtypeStruct((b, DIFF_W, nq), F32),
        scratch_shapes=[pltpu.VMEM((LANES, 2 * tq), BF16), pltpu.VMEM((8, 2 * tq), F32),
                        pltpu.VMEM((8, 2 * tq), F32), pltpu.VMEM((DIFF_VD, 2 * tq), F32)],
        compiler_params=pltpu.CompilerParams(dimension_semantics=("arbitrary",) * 4, vmem_limit_bytes=VMEM_LIMIT),
        name="diff_attn",
    )(lam.reshape(1, 1).astype(F32), qt, k, vt)


SSD_GROUPS_PER_STEP = 4


def _ssd_scan_kernel(x_ref, b_ref, c_ref, dt_ref, bias_ref, alog_ref, y_ref, st_ref):
    t = SSD_CHUNK
    d = pl.program_id(1)
    gstep = pl.program_id(2)
    ngrp = x_ref.shape[2] // SSD_GW

    @pl.when(pl.program_id(3) == 0)
    def _():
        st_ref[...] = jnp.zeros_like(st_ref)

    dt = _softplus(dt_ref[0] + bias_ref[...])
    dta = dt * (-jnp.exp(alog_ref[...]))
    tr = lax.broadcasted_iota(jnp.int32, (t, t), 0)
    tc = lax.broadcasted_iota(jnp.int32, (t, t), 1)
    fwd = d == 0
    incl = jnp.where(fwd, tc - tr, tr - tc) <= 0
    x_all, b_all, c_all, st_all = x_ref[0], b_ref[0], c_ref[0], st_ref[...]
    outs = {}
    chains = []
    for j in range(ngrp):
        xs = slice(j * SSD_GW, (j + 1) * SSD_GW)
        ns = slice(j * SSD_STATE, (j + 1) * SSD_STATE)
        first_head = d * SSD_HEADS + (gstep * ngrp + j) * (SSD_HEADS // SSD_GROUPS)
        chains.append(_ssd_chain(x_all[:, xs], b_all[:, ns], c_all[:, ns], st_all[:, xs], dt, dta, first_head, fwd,
                                 incl, outs, j))
    while chains:
        alive = []
        for ch in chains:
            if next(ch, _DONE) is not _DONE:
                alive.append(ch)
        chains = alive
    for j in range(ngrp):
        xs = slice(j * SSD_GW, (j + 1) * SSD_GW)
        y, st_new = outs[j]
        y_ref[0, 0, :, xs] = y
        st_ref[:, xs] = st_new


def _ssd_chain(x, bm, cm, st, dt, dta, first_head, fwd, incl, outs, key):
    t = SSD_CHUNK
    e_heads = SSD_HEADS // SSD_GROUPS
    lane = lax.broadcasted_iota(jnp.int32, (t, 2 * SSD_HEADS), 1)
    lane_x = lax.broadcasted_iota(jnp.int32, (t, SSD_GW), 1) // SSD_HD
    lane_e = lax.broadcasted_iota(jnp.int32, (t, LANES), 1)
    dt_x = jnp.zeros((t, SSD_GW), F32)
    dta4 = jnp.zeros((t, LANES), F32)
    for e in range(e_heads):
        pick = lane == first_head + e
        dt_e = jnp.sum(jnp.where(pick, dt, 0.0), axis=1, keepdims=True)
        dta_e = jnp.sum(jnp.where(pick, dta, 0.0), axis=1, keepdims=True)
        dt_x = jnp.where(lane_x == e, dt_e, dt_x)
        dta4 = jnp.where(lane_e == e, dta_e, dta4)

    cum4 = _mm_sel(incl.astype(BF16), dta4)
    cb = _mm_nt(cm, bm)
    y_off_raw = _mm(cm, st)
    yield
    cum_t = cum4.T
    last4 = jnp.where(fwd, cum4[t - 1:t, :], cum4[0:1, :])
    cum_x = jnp.zeros((t, SSD_GW), F32)
    last_x = jnp.zeros((1, SSD_GW), F32)
    for e in range(e_heads):
        cum_x = jnp.where(lane_x == e, cum4[:, e:e + 1], cum_x)
        last_x = jnp.where(lane_x[:1] == e, last4[:, e:e + 1], last_x)

    xdt = x * dt_x
    zero_h = jnp.zeros((t, LANES), F32)
    lane_h = lax.broadcasted_iota(jnp.int32, (t, LANES), 1) < SSD_HD
    ydiag = []
    for pair in range(e_heads // 2):
        lmats = []
        for e in (2 * pair, 2 * pair + 1):
            seg = cum4[:, e:e + 1] - cum_t[e:e + 1, :]
            lmats.append(cb * jnp.exp(jnp.where(incl, seg, NEG_BIG)))
        xp = xdt[:, pair * LANES:(pair + 1) * LANES]
        xbd = jnp.concatenate([jnp.where(lane_h, xp, zero_h), jnp.where(lane_h, zero_h, xp)], axis=0)
        ydiag.append(_mm(jnp.concatenate(lmats, axis=1), xbd))
    states = _mm_tn(bm, xdt * jnp.exp(last_x - cum_x))
    yield
    y = jnp.concatenate(ydiag, axis=1) + y_off_raw * jnp.exp(cum_x)
    outs[key] = (y, st * jnp.exp(last_x) + states)


def _ssd_scan(xbc, dt_raw, dt_bias, a_log, n_ctx):
    b, n_all, _ = xbc.shape
    t = SSD_CHUNK
    nc = n_all // t
    ncc = n_ctx // t
    cidx = lambda d, i: _chunk_order(d, i, ncc, nc)
    gg = SSD_GROUPS_PER_STEP
    nsteps_g = SSD_GROUPS // gg
    xw, nw = gg * SSD_GW, gg * SSD_STATE
    b0 = SSD_W // nw
    row = pl.BlockSpec((1, 2 * SSD_HEADS), lambda bi, d, g, i: (0, 0))
    return pl.pallas_call(
        _ssd_scan_kernel,
        grid=(b, 2, nsteps_g, nc),
        in_specs=[pl.BlockSpec((1, t, xw), lambda bi, d, g, i: (bi, cidx(d, i), g)),
                  pl.BlockSpec((1, t, nw), lambda bi, d, g, i: (bi, cidx(d, i), b0 + g)),
                  pl.BlockSpec((1, t, nw), lambda bi, d, g, i: (bi, cidx(d, i), b0 + nsteps_g + g)),
                  pl.BlockSpec((1, t, 2 * SSD_HEADS), lambda bi, d, g, i: (bi, cidx(d, i), 0)),
                  row, row],
        out_specs=pl.BlockSpec((1, 1, t, xw), lambda bi, d, g, i: (bi, d, cidx(d, i), g)),
        out_shape=jax.ShapeDtypeStruct((b, 2, n_all, SSD_W), F32),
        scratch_shapes=[pltpu.VMEM((SSD_STATE, xw), F32)],
        compiler_params=pltpu.CompilerParams(dimension_semantics=("arbitrary",) * 4, vmem_limit_bytes=VMEM_LIMIT),
        name="ssd_scan",
    )(xbc, xbc, xbc, dt_raw, dt_bias.reshape(1, -1), a_log.reshape(1, -1))


def _rms_norm(x, g, eps=EPS):
    return x * lax.rsqrt(jnp.mean(x * x, axis=-1, keepdims=True) + eps) * g


HALO = 8


def _dwconv_kernel(n_ctx, n_all, silu, cur_ref, prev_ref, next_ref, w_ref, b_ref, o_ref):
    tr = cur_ref.shape[1]
    taps = w_ref.shape[0]
    pad = taps // 2
    ext = jnp.concatenate([prev_ref[0], cur_ref[0], next_ref[0]], axis=0)
    t = pl.program_id(1) * tr + lax.broadcasted_iota(jnp.int32, (tr, 1), 0)
    acc = cur_ref[0] * w_ref[pad:pad + 1, :]
    for o in range(-pad, pad + 1):
        if o == 0:
            continue
        ts = t + o
        lo, hi = (t, ts) if o > 0 else (ts, t)
        crosses = (lo < n_ctx) & (hi >= n_ctx)
        valid = (ts >= 0) & (ts < n_all) & jnp.logical_not(crosses)
        src = ext[HALO + o:HALO + o + tr]
        acc = acc + jnp.where(valid, src, 0.0) * w_ref[o + pad:o + pad + 1, :]
    if silu:
        acc = acc + b_ref[...]
        acc = acc * _sigmoid(acc)
    o_ref[0] = acc


def _dwconv(u, col0, ncols, w, bias, n_ctx, silu):
    b, n_all, _ = u.shape
    tr = _pick(n_all, (768, 512, 384, 256, 128))
    wt = _pick(math.gcd(ncols, col0) if col0 else ncols, (512, 256, 128))
    c0 = col0 // wt
    hb = tr // HALO
    last = n_all // HALO - 1
    if bias is None:
        bias = jnp.zeros((ncols,), F32)
    return pl.pallas_call(
        functools.partial(_dwconv_kernel, n_ctx, n_all, silu),
        grid=(b, n_all // tr, ncols // wt),
        in_specs=[pl.BlockSpec((1, tr, wt), lambda bi, i, j: (bi, i, c0 + j)),
                  pl.BlockSpec((1, HALO, wt), lambda bi, i, j: (bi, jnp.maximum(i * hb - 1, 0), c0 + j)),
                  pl.BlockSpec((1, HALO, wt), lambda bi, i, j: (bi, jnp.minimum((i + 1) * hb, last), c0 + j)),
                  pl.BlockSpec((w.shape[0], wt), lambda bi, i, j: (0, j)),
                  pl.BlockSpec((1, wt), lambda bi, i, j: (0, j))],
        out_specs=pl.BlockSpec((1, tr, wt), lambda bi, i, j: (bi, i, j)),
        out_shape=jax.ShapeDtypeStruct((b, n_all, ncols), F32),
        compiler_params=pltpu.CompilerParams(dimension_semantics=("arbitrary",) * 3, vmem_limit_bytes=VMEM_LIMIT),
        name="dwconv",
    )(u, u, u, w, bias.reshape(1, ncols))


def _row_select(tbl_ref, i, tm, rows_per_batch, n_ctx):
    tiles_per_batch = rows_per_batch // tm
    bi = i // tiles_per_batch
    row = (i % tiles_per_batch) * tm + lax.broadcasted_iota(jnp.int32, (tm, 1), 0)
    nb = tbl_ref.shape[0] - 1
    return jnp.where(row < n_ctx, tbl_ref[nb:nb + 1, :], tbl_ref[pl.ds(bi, 1), :])


def _prenorm_kernel(n_ctx, rows_per_batch, x_ref, g_ref, shift_ref, scale_ref, o_ref):
    i = pl.program_id(0)
    tm = x_ref.shape[0]
    x = x_ref[...]
    y = x * lax.rsqrt(jnp.mean(x * x, axis=-1, keepdims=True) + EPS) * g_ref[...]
    scale = _row_select(scale_ref, i, tm, rows_per_batch, n_ctx)
    shift = _row_select(shift_ref, i, tm, rows_per_batch, n_ctx)
    o_ref[...] = (y * (1.0 + scale) + shift).astype(o_ref.dtype)


def _prenorm(x2d, g, shift_tbl, scale_tbl, n_ctx, rows_per_batch):
    m, n = x2d.shape
    tm = _pick(rows_per_batch, (384, 256, 128))
    tbl = pl.BlockSpec(shift_tbl.shape, lambda i: (0, 0))
    return pl.pallas_call(
        functools.partial(_prenorm_kernel, n_ctx, rows_per_batch),
        grid=(m // tm,),
        in_specs=[pl.BlockSpec((tm, n), lambda i: (i, 0)), pl.BlockSpec((1, n), lambda i: (0, 0)), tbl, tbl],
        out_specs=pl.BlockSpec((tm, n), lambda i: (i, 0)),
        out_shape=jax.ShapeDtypeStruct((m, n), BF16),
        compiler_params=pltpu.CompilerParams(dimension_semantics=("arbitrary",), vmem_limit_bytes=VMEM_LIMIT),
        name="prenorm",
    )(x2d, g.reshape(1, n), shift_tbl, scale_tbl)


def _rwkv_finish_kernel(y_ref, bonus_ref, g_ref, w_ref, b_ref, o_ref):
    y = y_ref[0, 0] + y_ref[0, 1]
    hr = lax.broadcasted_iota(jnp.int32, (LANES, LANES), 0) // RWKV_HD
    hc = lax.broadcasted_iota(jnp.int32, (LANES, LANES), 1) // RWKV_HD
    head_ones = (hr == hc).astype(BF16)
    outs = []
    for p in range(y.shape[1] // LANES):
        yp = y[:, p * LANES:(p + 1) * LANES]
        mu = _mm_sel_right(yp, head_ones) * (1.0 / RWKV_HD)
        dev = yp - mu
        var = _mm_sel_right(dev * dev, head_ones) * (1.0 / RWKV_HD)
        outs.append(dev * lax.rsqrt(var + RWKV_GN_EPS))
    yn = jnp.concatenate(outs, axis=1) * w_ref[...] + b_ref[...]
    g = g_ref[0]
    o_ref[0] = ((yn + bonus_ref[0]) * (g * _sigmoid(g))).astype(o_ref.dtype)


def _rwkv_finish(y2, bonus, ua, g_col0, lnx_w, lnx_b):
    b, _, n_all, _ = y2.shape
    tr = _pick(n_all, (768, 512, 384, 256, 128))
    wt = 256
    g0 = g_col0 // wt
    vec = pl.BlockSpec((1, wt), lambda bi, i, j: (0, j))
    blk = lambda c0: pl.BlockSpec((1, tr, wt), lambda bi, i, j: (bi, i, c0 + j))
    return pl.pallas_call(
        _rwkv_finish_kernel,
        grid=(b, n_all // tr, RWKV_W // wt),
        in_specs=[pl.BlockSpec((1, 2, tr, wt), lambda bi, i, j: (bi, 0, i, j)), blk(0), blk(g0), vec, vec],
        out_specs=blk(0),
        out_shape=jax.ShapeDtypeStruct((b, n_all, RWKV_W), BF16),
        compiler_params=pltpu.CompilerParams(dimension_semantics=("arbitrary",) * 3, vmem_limit_bytes=VMEM_LIMIT),
        name="rwkv_finish",
    )(y2, bonus, ua, lnx_w.reshape(1, -1), lnx_b.reshape(1, -1))


def _ssd_finish_kernel(y_ref, x_ref, z_ref, d_ref, g_ref, o_ref):
    z = z_ref[0]
    y = (y_ref[0, 0] + y_ref[0, 1] + d_ref[...] * x_ref[0]) * (z * _sigmoid(z))
    outs = []
    for grp in range(y.shape[1] // SSD_GW):
        yg = y[:, grp * SSD_GW:(grp + 1) * SSD_GW]
        outs.append(yg * lax.rsqrt(jnp.mean(yg * yg, axis=-1, keepdims=True) + EPS))
    o_ref[0] = (jnp.concatenate(outs, axis=1) * g_ref[...]).astype(o_ref.dtype)


def _ssd_finish(y2, xact, uc, d_vec, norm_g):
    b, _, n_all, _ = y2.shape
    tr = _pick(n_all, (384, 256, 128))
    wt = 1024
    vec = pl.BlockSpec((1, wt), lambda bi, i, j: (0, j))
    blk = pl.BlockSpec((1, tr, wt), lambda bi, i, j: (bi, i, j))
    return pl.pallas_call(
        _ssd_finish_kernel,
        grid=(b, n_all // tr, SSD_W // wt),
        in_specs=[pl.BlockSpec((1, 2, tr, wt), lambda bi, i, j: (bi, 0, i, j)), blk, blk, vec, vec],
        out_specs=blk,
        out_shape=jax.ShapeDtypeStruct((b, n_all, SSD_W), BF16),
        compiler_params=pltpu.CompilerParams(dimension_semantics=("arbitrary",) * 3, vmem_limit_bytes=VMEM_LIMIT),
        name="ssd_finish",
    )(y2, xact, uc, d_vec.reshape(1, -1), norm_g.reshape(1, -1))


def _axial_rope(n_tok):
    rows = n_tok // GRID_W
    row = jnp.repeat(jnp.arange(rows, dtype=F32), GRID_W)
    col = jnp.tile(jnp.arange(GRID_W, dtype=F32), rows)
    n_freq = DIFF_HD // 4
    inv = ROPE_BASE ** (-jnp.arange(n_freq, dtype=F32) / n_freq)
    ang = jnp.concatenate([row[:, None] * inv, col[:, None] * inv], axis=-1)
    return jnp.cos(ang), jnp.sin(ang)


def _apply_rope(t, cos, sin):
    b, n, _ = t.shape
    t5 = t.reshape(b, n, DIFF_HEADS, 2, DIFF_HD)
    half = DIFF_HD // 2
    c = cos[None, :, None, None, :]
    s = sin[None, :, None, None, :]
    t1, t2 = t5[..., :half], t5[..., half:]
    return jnp.concatenate([t1 * c - t2 * s, t2 * c + t1 * s], axis=-1).reshape(b, n, DIFF_W)


def _rwkv_branch(ua, n_ctx, conv_w, w0, w2, a0, a2, k_k, k_a, r_k, lnx_w, lnx_b):
    uconv = _dwconv(ua, 0, RWKV_CONV_CH, conv_w, None, n_ctx, False)
    m, n, q, yl, bonus = _rwkv_local(uconv, w0, w2, a0, a2, k_k, k_a, r_k)
    y2 = _rwkv_scan(m, n, q, yl, n_ctx)
    return _rwkv_finish(y2, bonus, ua, RWKV_CONV_CH, lnx_w, lnx_b)


def _diff_branch(q, k, v, g, n_ctx, cos, sin, lam_p, subln_g, lam_init):
    b, n_all, _ = q.shape
    lam = jnp.exp(jnp.sum(lam_p[0] * lam_p[1])) - jnp.exp(jnp.sum(lam_p[2] * lam_p[3])) + lam_init
    scale = DIFF_HD ** -0.5 * math.log2(math.e)
    qc, ql = q[:, :n_ctx], _apply_rope(q[:, n_ctx:], cos, sin)
    kc, kl = k[:, :n_ctx], _apply_rope(k[:, n_ctx:], cos, sin)
    k_all = jnp.concatenate([kc, kl], axis=1).astype(BF16)
    vt_all = jnp.swapaxes(v, 1, 2).astype(BF16)
    feat_major = lambda t: jnp.swapaxes(t * scale, 1, 2).astype(BF16)
    oc = _diff_attn(feat_major(qc), k_all[:, :n_ctx], vt_all[:, :, :n_ctx], lam)
    ol = _diff_attn(feat_major(ql), k_all, vt_all, lam)
    o = jnp.swapaxes(jnp.concatenate([oc, ol], axis=2), 1, 2).reshape(b, n_all, DIFF_HEADS, DIFF_VD)
    o = _rms_norm(o, subln_g) * (1.0 - lam_init)
    return o.reshape(b, n_all, DIFF_W) * jax.nn.silu(g)


def _ssd_branch(uc, dt_raw, n_ctx, conv_w, conv_b, a_log, dt_bias, d_skip, norm_g):
    xact = _dwconv(uc, SSD_W, SSD_XBC, conv_w, conv_b, n_ctx, True)
    y2 = _ssd_scan(xact, dt_raw, dt_bias, a_log, n_ctx)
    return _ssd_finish(y2, xact, uc, jnp.repeat(d_skip, SSD_HD), norm_g)


def kernel(x, c, ctx, c_ctx, w_mod, b_mod, g_pre, g_post, w_in, rwkv_conv, rwkv_w0, rwkv_w2, rwkv_a0, rwkv_a2,
           rwkv_k_k, rwkv_k_a, rwkv_r_k, rwkv_lnx_w, rwkv_lnx_b, diff_lambda, diff_subln, ssd_conv_w, ssd_conv_b,
           ssd_a_log, ssd_dt_bias, ssd_d, ssd_norm, w_branch_a, w_branch_b, w_branch_c, w_out):
    b, n_lat, dm = x.shape
    n_ctx = ctx.shape[1]
    n_all = n_ctx + n_lat
    depth = w_in.shape[0]
    cos, sin = _axial_rope(n_lat)
    cond = jax.nn.silu(jnp.concatenate([c, c_ctx[None, :]], axis=0))
    cond = jnp.pad(cond, ((0, 8 - (b + 1)), (0, 0)))

    o_rw, o_g, o_dt = 0, RWKV_CONV_CH, RWKV_CONV_CH + RWKV_W
    src_dt = o_dt + 4 * DIFF_W + SSD_W + SSD_XBC
    pad_a = -(o_dt + 2 * SSD_HEADS) % 512

    xa = jnp.concatenate([ctx, x], axis=1)
    for li in range(depth):
        lam_init = 0.8 - 0.6 * math.exp(-0.3 * li)
        mod = _matmul(cond, w_mod, tn_cands=(1024, 512, 256, 128), layer=li)[:b + 1] + b_mod[li]
        shift_l, scale_l, gate_l = jnp.split(mod[:b], 3, axis=-1)
        shift_c, scale_c, gate_c = jnp.split(mod[b], 3, axis=-1)
        tbl = lambda vl, vc: jnp.concatenate([vl, vc[None, :]], axis=0)
        hb = _prenorm(xa.reshape(b * n_all, dm), g_pre[li], tbl(shift_l, shift_c), tbl(scale_l, scale_c), n_ctx,
                      n_all)
        wl = w_in[li]
        proj = lambda w: _matmul(hb, w.astype(BF16)).reshape(b, n_all, -1)
        ua = proj(jnp.concatenate([wl[:, :o_dt], wl[:, src_dt:src_dt + 2 * SSD_HEADS],
                                   jnp.zeros((dm, pad_a), wl.dtype)], axis=1))
        ub = proj(wl[:, o_dt:o_dt + 4 * DIFF_W])
        uc = proj(wl[:, o_dt + 4 * DIFF_W:src_dt])
        ug = proj(wl[:, src_dt + 2 * SSD_HEADS:])

        oa = _rwkv_branch(ua, n_ctx, rwkv_conv[li], rwkv_w0[li], rwkv_w2[li], rwkv_a0[li], rwkv_a2[li],
                          rwkv_k_k[li], rwkv_k_a[li], rwkv_r_k[li], rwkv_lnx_w[li], rwkv_lnx_b[li])
        ob = _diff_branch(ub[..., :DIFF_W], ub[..., DIFF_W:2 * DIFF_W], ub[..., 2 * DIFF_W:3 * DIFF_W],
                          ub[..., 3 * DIFF_W:], n_ctx, cos, sin, diff_lambda[li], diff_subln[li], lam_init)
        oc = _ssd_branch(uc, ua[..., o_dt:o_dt + 2 * SSD_HEADS], n_ctx, ssd_conv_w[li], ssd_conv_b[li],
                         ssd_a_log[li], ssd_dt_bias[li], ssd_d[li], ssd_norm[li])

        flat = lambda t: t.astype(BF16).reshape(b * n_all, -1)
        mrg = _merge(flat(oa), flat(ob), flat(oc), w_branch_a[li].astype(BF16), w_branch_b[li].astype(BF16),
                     w_branch_c[li].astype(BF16), ug.reshape(b * n_all, 3 * dm))
        xa = _out_norm(mrg, w_out[li].astype(BF16), xa.reshape(b * n_all, dm), tbl(gate_l, gate_c), g_post[li],
                       n_ctx, n_all).reshape(b, n_all, dm)
    return xa[:, n_ctx:]
```

```python
import functools
import math

import jax
import jax.numpy as jnp
from jax import lax
from jax.experimental import pallas as pl
from jax.experimental.pallas import tpu as pltpu

F32 = jnp.float32
BF16 = jnp.bfloat16

EPS = 1e-6
GRID_W = 64
ROPE_BASE = 10000.0

RWKV_HEADS = 16
RWKV_HD = 64
RWKV_W = RWKV_HEADS * RWKV_HD
RWKV_LORA = 64
RWKV_CONV_CH = 3 * RWKV_W + 4 * RWKV_LORA
RWKV_GN_EPS = 64e-5
RWKV_CHUNK = 64
RWKV_LOCAL_WIDTH = 1024

DIFF_HEADS = 8
DIFF_HD = 64
DIFF_VD = 2 * DIFF_HD
DIFF_W = DIFF_HEADS * DIFF_VD

SSD_HEADS = 32
SSD_HD = 64
SSD_W = SSD_HEADS * SSD_HD
SSD_GROUPS = 8
SSD_STATE = 128
SSD_CHUNK = 128
SSD_XBC = SSD_W + 2 * SSD_GROUPS * SSD_STATE
SSD_GW = SSD_W // SSD_GROUPS

LANES = 128
VMEM_LIMIT = 56 * 1024 * 1024

NEG_BIG = -1e30


def _pick(n, candidates):
    for c in candidates:
        if n % c == 0:
            return c
    return n


def _mm(a, b):
    return jnp.dot(a.astype(BF16), b.astype(BF16), preferred_element_type=F32)


def _mm_nt(a, b):
    return lax.dot_general(a.astype(BF16), b.astype(BF16), (((1,), (1,)), ((), ())), preferred_element_type=F32)


def _mm_tn(a, b):
    return lax.dot_general(a.astype(BF16), b.astype(BF16), (((0,), (0,)), ((), ())), preferred_element_type=F32)


def _split3(x):
    x1 = x.astype(BF16)
    r1 = x - x1.astype(F32)
    x2 = r1.astype(BF16)
    x3 = (r1 - x2.astype(F32)).astype(BF16)
    return x1, x2, x3


def _mm_sel(sel, x):
    x1, x2, x3 = _split3(x)
    d = lambda y: jnp.dot(sel, y, preferred_element_type=F32)
    return d(x1) + d(x2) + d(x3)


def _softplus(x):
    return jnp.maximum(x, 0.0) + jnp.log(1.0 + jnp.exp(-jnp.abs(x)))


def _sigmoid(x):
    return 1.0 / (1.0 + jnp.exp(-x))


def _matmul_kernel(a_ref, w_ref, o_ref):
    o_ref[...] = _mm(a_ref[...], w_ref[...]).astype(o_ref.dtype)


def _matmul(a, w, out_dtype=F32, tm_cands=(768, 512, 384, 256, 128, 8), tn_cands=(512, 256, 128), layer=None):
    m, k = a.shape
    n = w.shape[-1]
    tm = _pick(m, tm_cands)
    tn = _pick(n, tn_cands)
    if layer is None:
        w_spec = pl.BlockSpec((k, tn), lambda i, j: (0, j))
    else:
        w_spec = pl.BlockSpec((None, k, tn), lambda i, j: (layer, 0, j))
    return pl.pallas_call(
        _matmul_kernel,
        grid=(m // tm, n // tn),
        in_specs=[pl.BlockSpec((tm, k), lambda i, j: (i, 0)), w_spec],
        out_specs=pl.BlockSpec((tm, tn), lambda i, j: (i, j)),
        out_shape=jax.ShapeDtypeStruct((m, n), out_dtype),
        compiler_params=pltpu.CompilerParams(dimension_semantics=("arbitrary", "arbitrary"),
                                             vmem_limit_bytes=VMEM_LIMIT),
        name="dense_matmul",
    )(a, w)


def _merge_kernel(oa_ref, ob_ref, oc_ref, wa_ref, wb_ref, wc_ref, sa_ref, sb_ref, sc_ref, o_ref):
    dot = lambda x_ref, w_ref: jnp.dot(x_ref[...], w_ref[...], preferred_element_type=F32)
    o = (_sigmoid(sa_ref[...]) * dot(oa_ref, wa_ref) + _sigmoid(sb_ref[...]) * dot(ob_ref, wb_ref)
         + _sigmoid(sc_ref[...]) * dot(oc_ref, wc_ref))
    o_ref[...] = o.astype(o_ref.dtype)


def _merge(oa, ob, oc, wa, wb, wc, gates):
    m = oa.shape[0]
    n = wa.shape[1]
    tm = _pick(m, (768, 512, 384, 256, 128))
    tn = _pick(n, (512, 256, 128))
    nj = n // tn
    act = lambda t: pl.BlockSpec((tm, t.shape[1]), lambda i, j: (i, 0))
    wgt = lambda t: pl.BlockSpec((t.shape[0], tn), lambda i, j: (0, j))
    gate = lambda s: pl.BlockSpec((tm, tn), lambda i, j: (i, s * nj + j))
    return pl.pallas_call(
        _merge_kernel,
        grid=(m // tm, nj),
        in_specs=[act(oa), act(ob), act(oc), wgt(wa), wgt(wb), wgt(wc), gate(0), gate(1), gate(2)],
        out_specs=pl.BlockSpec((tm, tn), lambda i, j: (i, j)),
        out_shape=jax.ShapeDtypeStruct((m, n), BF16),
        compiler_params=pltpu.CompilerParams(dimension_semantics=("arbitrary", "arbitrary"),
                                             vmem_limit_bytes=VMEM_LIMIT),
        name="branch_merge",
    )(oa, ob, oc, wa, wb, wc, gates, gates, gates)


def _out_norm_kernel(n_ctx, rows_per_batch, m_ref, w_ref, x_ref, gate_ref, g_ref, o_ref, acc_ref):
    i = pl.program_id(0)
    kk = pl.program_id(1)

    @pl.when(kk == 0)
    def _():
        acc_ref[...] = jnp.zeros_like(acc_ref)

    acc_ref[...] += jnp.dot(m_ref[...], w_ref[...], preferred_element_type=F32)

    @pl.when(kk == pl.num_programs(1) - 1)
    def _():
        tm = acc_ref.shape[0]
        out = acc_ref[...]
        y = out * lax.rsqrt(jnp.mean(out * out, axis=-1, keepdims=True) + EPS) * g_ref[...]
        tiles_per_batch = rows_per_batch // tm
        bi = i // tiles_per_batch
        row = (i % tiles_per_batch) * tm + lax.broadcasted_iota(jnp.int32, (tm, 1), 0)
        nb = gate_ref.shape[0] - 1
        gate = jnp.where(row < n_ctx, gate_ref[nb:nb + 1, :], gate_ref[pl.ds(bi, 1), :])
        o_ref[...] = x_ref[...] + gate * y


def _out_norm(mrg, w, x2d, gates, g_post, n_ctx, rows_per_batch):
    m, k = mrg.shape
    n = w.shape[1]
    tm = _pick(rows_per_batch, (384, 256, 128))
    tk = _pick(k, (512, 256, 128))
    return pl.pallas_call(
        functools.partial(_out_norm_kernel, n_ctx, rows_per_batch),
        grid=(m // tm, k // tk),
        in_specs=[pl.BlockSpec((tm, tk), lambda i, kk: (i, kk)),
                  pl.BlockSpec((tk, n), lambda i, kk: (kk, 0)),
                  pl.BlockSpec((tm, n), lambda i, kk: (i, 0)),
                  pl.BlockSpec(gates.shape, lambda i, kk: (0, 0)),
                  pl.BlockSpec((1, n), lambda i, kk: (0, 0))],
        out_specs=pl.BlockSpec((tm, n), lambda i, kk: (i, 0)),
        out_shape=jax.ShapeDtypeStruct((m, n), F32),
        scratch_shapes=[pltpu.VMEM((tm, n), F32)],
        compiler_params=pltpu.CompilerParams(dimension_semantics=("arbitrary", "arbitrary"),
                                             vmem_limit_bytes=VMEM_LIMIT),
        name="out_norm",
    )(mrg, w, x2d, gates, g_post.reshape(1, n))


def _head_masks(rows):
    lane = lax.broadcasted_iota(jnp.int32, (rows, LANES), 1)
    return lane < RWKV_HD


def _bd(x, first):
    zero = jnp.zeros_like(x)
    return jnp.concatenate([jnp.where(first, x, zero), jnp.where(first, zero, x)], axis=0)


def _rwkv_local_kernel(r_ref, k_ref, v_ref, lw_ref, la_ref, w0_ref, w2_ref, a0_ref, a2_ref, kk_ref, ka_ref,
                       rk_ref, m_ref, n_ref, q_ref, yl_ref, bonus_ref):
    c = RWKV_CHUNK
    width = r_ref.shape[2]
    first = _head_masks(c)
    row = lax.broadcasted_iota(jnp.int32, (c, LANES), 0)
    col = lax.broadcasted_iota(jnp.int32, (c, LANES), 1) % RWKV_HD
    eye_cat = (row == col).astype(F32)
    hr = lax.broadcasted_iota(jnp.int32, (LANES, LANES), 0) // RWKV_HD
    hc = lax.broadcasted_iota(jnp.int32, (LANES, LANES), 1) // RWKV_HD
    head_ones = (hr == hc).astype(BF16)
    tr = lax.broadcasted_iota(jnp.int32, (c, c), 0)
    tc = lax.broadcasted_iota(jnp.int32, (c, c), 1)
    zrow = jnp.zeros((RWKV_LORA, width), F32)

    tanh_lw = jnp.tanh(lw_ref[0])
    la = la_ref[0]
    w_raw2, a_sig2 = [], []
    for d in (0, 1):
        pad = (lambda m: jnp.concatenate([m, zrow], axis=0)) if d == 0 else (lambda m: jnp.concatenate([zrow, m], axis=0))
        w_raw2.append(w0_ref[d:d + 1, :] + _mm(tanh_lw, pad(w2_ref[d])))
        a_sig2.append(_sigmoid(a0_ref[d:d + 1, :] + _mm(la, pad(a2_ref[d]))))

    r_all, k_all, v_all = r_ref[0], k_ref[0], v_ref[0]
    kk_all, ka_all, rk_all = kk_ref[...], ka_ref[...], rk_ref[...]
    consts = (first, row, col, eye_cat, tr, tc)
    chains, outs = [], {}
    for pi in range(width // LANES):
        sl = slice(pi * LANES, (pi + 1) * LANES)
        r, k, v = r_all[:, sl], k_all[:, sl], v_all[:, sl]
        kkp = k * kk_all[:, sl]
        kk = kkp / jnp.maximum(jnp.sqrt(_mm_sel_right(kkp * kkp, head_ones)), 1e-12)
        a_sum = a_sig2[0][:, sl] + a_sig2[1][:, sl]
        kd_sum = k * (2.0 + (a_sum - 2.0) * ka_all[:, sl])
        outs[pi] = _mm_sel_right(r * kd_sum * rk_all[:, sl], head_ones) * v
        for d in (0, 1):
            chains.append(_rwkv_chain(d, r, k, v, kk, w_raw2[d][:, sl], a_sig2[d][:, sl], ka_all[:, sl], consts,
                                      outs, (pi, d)))
    while chains:
        alive = []
        for ch in chains:
            if next(ch, _DONE) is not _DONE:
                alive.append(ch)
        chains = alive
    for pi in range(width // LANES):
        sl = slice(pi * LANES, (pi + 1) * LANES)
        for d in (0, 1):
            m_cat, n_cat, q, yloc = outs[(pi, d)]
            m_ref[0, d, 0, :, sl] = m_cat
            n_ref[0, d, 0, :, sl] = n_cat
            q_ref[0, d, :, sl] = q
            yl_ref[0, d, :, sl] = yloc
        bonus_ref[0, :, sl] = outs[pi]


_DONE = object()


def _rwkv_chain(d, r, k, v, kk, w_raw, a_sig, ka, consts, outs, key):
    c = RWKV_CHUNK
    first, row, col, eye_cat, tr, tc = consts
    if d == 0:
        incl, strict, tri = col <= row, col < row, (tc <= tr)
    else:
        incl, strict, tri = col >= row, col > row, (tc >= tr)
    logw = -jnp.exp(-_softplus(-w_raw) - 0.5)
    k_dir = k * (1.0 + (a_sig - 1.0) * ka)
    b_dir = kk * a_sig

    cum = _mm_sel(tri.astype(BF16), logw)
    yield
    tot = cum[c - 1:c, :] if d == 0 else cum[0:1, :]
    g_incl = jnp.exp(cum)
    g_excl = jnp.exp(cum - logw)
    g_inv = jnp.exp(-cum)
    g_hat = jnp.exp(tot - cum)
    at = -kk * g_excl
    rt = r * g_incl
    bt = b_dir * g_inv
    kt = k_dir * g_inv
    bh = b_dir * g_hat
    kh = k_dir * g_hat

    sc = _mm_nt(jnp.concatenate([at, rt], axis=0),
                jnp.concatenate([_bd(bt, first), _bd(kt, first)], axis=0))
    yield
    zero = jnp.zeros((c, LANES), F32)
    a_ab = jnp.where(strict, sc[:c, :LANES], zero)
    a_ak = jnp.where(strict, sc[:c, LANES:], zero)
    a_rb = jnp.where(incl, sc[c:, :LANES], zero)
    a_rk = jnp.where(incl, sc[c:, LANES:], zero)

    t = eye_cat + a_ab
    p = _mm(a_ab, _bd(a_ab, first))
    akv = _mm(a_ak, _bd(v, first))
    yield
    for _ in range(4):
        pbd = _bd(p, first)
        t, p = t + _mm(t, pbd), _mm(p, pbd)
        yield
    t = t + _mm(t, _bd(p, first))
    yield

    x = _mm(t, jnp.concatenate([_bd(akv, first), _bd(at, first)], axis=1))
    yield
    uloc = x[:, :LANES]
    w = x[:, LANES:]
    yloc = _mm(jnp.concatenate([a_rb, a_rk], axis=1),
               jnp.concatenate([_bd(uloc, first), _bd(v, first)], axis=0))
    q = rt + _mm(a_rb, _bd(w, first))
    g = _mm_tn(jnp.concatenate([bh, kh], axis=1),
               jnp.concatenate([uloc, w, v], axis=1))
    yield
    nfull = g[:LANES, :LANES] + g[LANES:, 2 * LANES:]
    mfull = g[:LANES, LANES:2 * LANES]
    n_cat = jnp.where(first, nfull[:c], zero) + jnp.where(first, zero, nfull[c:])
    m_cat = jnp.where(first, mfull[:c], zero) + jnp.where(first, zero, mfull[c:]) + eye_cat * jnp.exp(tot)
    outs[key] = (m_cat, n_cat, q, yloc)


def _mm_sel_right(x, sel):
    x1, x2, x3 = _split3(x)
    d = lambda y: jnp.dot(y, sel, preferred_element_type=F32)
    return d(x1) + d(x2) + d(x3)


def _rwkv_local(uconv, w0, w2, a0, a2, k_k, k_a, r_k):
    b, n_all, _ = uconv.shape
    c = RWKV_CHUNK
    nc = n_all // c
    wb = RWKV_LOCAL_WIDTH
    nblk = RWKV_W // wb
    row = lambda t: t.reshape(1, RWKV_W)
    col_blk = lambda off: pl.BlockSpec((1, c, wb), lambda bi, ci, pi: (bi, ci, off + pi))
    lora_blk = lambda off: pl.BlockSpec((1, c, LANES), lambda bi, ci, pi: (bi, ci, off))
    vec2 = pl.BlockSpec((2, wb), lambda bi, ci, pi: (0, pi))
    mat2 = pl.BlockSpec((2, RWKV_LORA, wb), lambda bi, ci, pi: (0, 0, pi))
    vec1 = pl.BlockSpec((1, wb), lambda bi, ci, pi: (0, pi))
    mn_shape = jax.ShapeDtypeStruct((b, 2, nc, RWKV_HD, RWKV_W), F32)
    mn_spec = pl.BlockSpec((1, 2, 1, RWKV_HD, wb), lambda bi, ci, pi: (bi, 0, ci, 0, pi))
    qy_shape = jax.ShapeDtypeStruct((b, 2, n_all, RWKV_W), F32)
    qy_spec = pl.BlockSpec((1, 2, c, wb), lambda bi, ci, pi: (bi, 0, ci, pi))
    lora0 = 3 * RWKV_W // LANES
    return pl.pallas_call(
        _rwkv_local_kernel,
        grid=(b, nc, nblk),
        in_specs=[col_blk(0), col_blk(nblk), col_blk(2 * nblk), lora_blk(lora0), lora_blk(lora0 + 1),
                  vec2, mat2, vec2, mat2, vec1, vec1, vec1],
        out_specs=[mn_spec, mn_spec, qy_spec, qy_spec,
                   pl.BlockSpec((1, c, wb), lambda bi, ci, pi: (bi, ci, pi))],
        out_shape=[mn_shape, mn_shape, qy_shape, qy_shape, jax.ShapeDtypeStruct((b, n_all, RWKV_W), F32)],
        compiler_params=pltpu.CompilerParams(dimension_semantics=("arbitrary",) * 3, vmem_limit_bytes=VMEM_LIMIT),
        name="rwkv_local",
    )(uconv, uconv, uconv, uconv, uconv, w0, w2, a0, a2, row(k_k), row(k_a), row(r_k))


def _rwkv_scan_kernel(mf_ref, nf_ref, qf_ref, ylf_ref, mb_ref, nb_ref, qb_ref, ylb_ref, yf_ref, yb_ref, s_ref):
    c = RWKV_CHUNK

    @pl.when(pl.program_id(1) == 0)
    def _():
        s_ref[...] = jnp.zeros_like(s_ref)

    first = _head_masks(RWKV_HD)
    dot = lambda x, y: jnp.dot(x, y, preferred_element_type=F32)
    for d, (m_ref, n_ref, q_ref, yl_ref, y_ref) in enumerate(((mf_ref, nf_ref, qf_ref, ylf_ref, yf_ref),
                                                             (mb_ref, nb_ref, qb_ref, ylb_ref, yb_ref))):
        for p in range(RWKV_W // LANES):
            sl = slice(p * LANES, (p + 1) * LANES)
            s1, s2, s3 = _split3(_bd(s_ref[d, :, sl], first))
            lhs = jnp.concatenate([q_ref[0, 0, :, sl], m_ref[0, 0, 0, :, sl]], axis=0)
            l1, l2, l3 = _split3(lhs)
            z = dot(l1, s1) + (dot(l1, s2) + dot(l2, s1)) + (dot(l2, s2) + dot(l1, s3) + dot(l3, s1))
            y_ref[0, :, sl] = yl_ref[0, 0, :, sl] + z[:c]
            s_ref[d, :, sl] = z[c:] + n_ref[0, 0, 0, :, sl]


def _chunk_order(d, i, n_ctx_chunks, n_chunks):
    back = jnp.where(i < n_ctx_chunks, n_ctx_chunks - 1 - i, n_chunks + n_ctx_chunks - 1 - i)
    return jnp.where(d == 0, i, back)


def _rwkv_scan(m, n, q, yl, n_ctx):
    b, _, nc, _, _ = m.shape
    n_all = q.shape[2]
    c = RWKV_CHUNK
    ncc = n_ctx // c
    cidx = lambda d, i: _chunk_order(d, i, ncc, nc)
    mn_spec = lambda d: pl.BlockSpec((1, 1, 1, RWKV_HD, RWKV_W), lambda bi, i: (bi, d, cidx(d, i), 0, 0))
    qy_spec = lambda d: pl.BlockSpec((1, 1, c, RWKV_W), lambda bi, i: (bi, d, cidx(d, i), 0))
    y_spec = lambda d: pl.BlockSpec((1, c, RWKV_W), lambda bi, i: (bi, cidx(d, i), 0))
    y_shape = jax.ShapeDtypeStruct((b, n_all, RWKV_W), F32)
    return pl.pallas_call(
        _rwkv_scan_kernel,
        grid=(b, nc),
        in_specs=[mn_spec(0), mn_spec(0), qy_spec(0), qy_spec(0), mn_spec(1), mn_spec(1), qy_spec(1), qy_spec(1)],
        out_specs=[y_spec(0), y_spec(1)],
        out_shape=[y_shape, y_shape],
        scratch_shapes=[pltpu.VMEM((2, RWKV_HD, RWKV_W), F32)],
        compiler_params=pltpu.CompilerParams(dimension_semantics=("arbitrary",) * 2, vmem_limit_bytes=VMEM_LIMIT),
        name="rwkv_scan",
    )(m, n, q, yl, m, n, q, yl)


ATTN_COLS = 256


def _diff_attn_kernel(tsub, lam_ref, qt_ref, k_ref, vt_ref, o_ref, qs_ref, m_ref, l_ref, acc_ref):
    tq = qt_ref.shape[2]
    tk = k_ref.shape[1]
    kv = pl.program_id(3)

    @pl.when(kv == 0)
    def _():
        qt = qt_ref[0]
        feat = lax.broadcasted_iota(jnp.int32, qt.shape, 0)
        zero = jnp.zeros_like(qt)
        qs_ref[...] = jnp.concatenate([jnp.where(feat < DIFF_HD, qt, zero), jnp.where(feat < DIFF_HD, zero, qt)], axis=1)
        m_ref[...] = jnp.full_like(m_ref, NEG_BIG)
        l_ref[...] = jnp.zeros_like(l_ref)
        acc_ref[...] = jnp.zeros_like(acc_ref)

    nsub = tk // tsub
    strips = [slice(cb * ATTN_COLS, (cb + 1) * ATTN_COLS) for cb in range(2 * tq // ATTN_COLS)]
    ones = jnp.ones((ATTN_ONES_ROWS, tsub), BF16)

    def scores_of(j):
        kj = k_ref[0, j * tsub:(j + 1) * tsub, :]
        return [jnp.dot(kj, qs_ref[:, sl], preferred_element_type=F32) for sl in strips]

    nxt = scores_of(0)
    for j in range(nsub):
        scores = nxt
        if j + 1 < nsub:
            nxt = scores_of(j + 1)
        v_ext = jnp.concatenate([vt_ref[0, :, j * tsub:(j + 1) * tsub], ones], axis=0)
        for sl, s in zip(strips, scores):
            m_old = m_ref[:, sl]
            m_new = jnp.maximum(m_old, jnp.max(s, axis=0, keepdims=True))
            alpha = jnp.exp2(m_old - m_new)
            p = jnp.exp2(s - m_new[:1]).astype(BF16)
            pv = jnp.dot(v_ext, p, preferred_element_type=F32)
            acc_ref[:, sl] = alpha[:1] * acc_ref[:, sl] + pv[:DIFF_VD]
            l_ref[:, sl] = alpha * l_ref[:, sl] + pv[DIFF_VD:DIFF_VD + 8]
            m_ref[:, sl] = m_new

    @pl.when(kv == pl.num_programs(3) - 1)
    def _():
        o = acc_ref[...] / l_ref[:1, :]
        o_ref[0] = o[:, :tq] - lam_ref[0, 0] * o[:, tq:]


ATTN_ONES_ROWS = 16
ATTN_MAX_SUBTILES = 11


def _diff_attn(qt, k, vt, lam):
    b, _, nq = qt.shape
    nk = k.shape[1]
    tq = _pick(nq, (512, 256, 128))
    tsub = _pick(nk, (768, 512, 384, 256, 128))
    n_sub = nk // tsub
    tk = tsub * max(s for s in range(1, ATTN_MAX_SUBTILES + 1) if n_sub % s == 0)
    return pl.pallas_call(
        functools.partial(_diff_attn_kernel, tsub),
        grid=(b, DIFF_HEADS, nq // tq, nk // tk),
        in_specs=[pl.BlockSpec(memory_space=pltpu.SMEM),
                  pl.BlockSpec((1, LANES, tq), lambda bi, h, i, j: (bi, h, i)),
                  pl.BlockSpec((1, tk, LANES), lambda bi, h, i, j: (bi, j, h)),
                  pl.BlockSpec((1, LANES, tk), lambda bi, h, i, j: (bi, h, j))],
        out_specs=pl.BlockSpec((1, LANES, tq), lambda bi, h, i, j: (bi, h, i)),
        out_shape=jax.ShapeDtypeStruct((b, DIFF_W, nq), F32),
        scratch_shapes=[pltpu.VMEM((LANES, 2 * tq), BF16), pltpu.VMEM((8, 2 * tq), F32),
                        pltpu.VMEM((8, 2 * tq), F32), pltpu.VMEM((DIFF_VD, 2 * tq), F32)],
        compiler_params=pltpu.CompilerParams(dimension_semantics=("arbitrary",) * 4, vmem_limit_bytes=VMEM_LIMIT),
        name="diff_attn",
    )(lam.reshape(1, 1).astype(F32), qt, k, vt)


SSD_GROUPS_PER_STEP = 8


def _ssd_scan_kernel(x_ref, b_ref, c_ref, dt_ref, bias_ref, alog_ref, y_ref, st_ref):
    t = SSD_CHUNK
    d = pl.program_id(1)
    gstep = pl.program_id(2)
    ngrp = x_ref.shape[2] // SSD_GW

    @pl.when(pl.program_id(3) == 0)
    def _():
        st_ref[...] = jnp.zeros_like(st_ref)

    dt = _softplus(dt_ref[0] + bias_ref[...])
    dta = dt * (-jnp.exp(alog_ref[...]))
    tr = lax.broadcasted_iota(jnp.int32, (t, t), 0)
    tc = lax.broadcasted_iota(jnp.int32, (t, t), 1)
    fwd = d == 0
    incl = jnp.where(fwd, tc - tr, tr - tc) <= 0
    x_all, b_all, c_all, st_all = x_ref[0], b_ref[0], c_ref[0], st_ref[...]
    outs = {}
    chains = []
    for j in range(ngrp):
        xs = slice(j * SSD_GW, (j + 1) * SSD_GW)
        ns = slice(j * SSD_STATE, (j + 1) * SSD_STATE)
        first_head = d * SSD_HEADS + (gstep * ngrp + j) * (SSD_HEADS // SSD_GROUPS)
        chains.append(_ssd_chain(x_all[:, xs], b_all[:, ns], c_all[:, ns], st_all[:, xs], dt, dta, first_head, fwd,
                                 incl, outs, j))
    while chains:
        alive = []
        for ch in chains:
            if next(ch, _DONE) is not _DONE:
                alive.append(ch)
        chains = alive
    for j in range(ngrp):
        xs = slice(j * SSD_GW, (j + 1) * SSD_GW)
        y, st_new = outs[j]
        y_ref[0, 0, :, xs] = y
        st_ref[:, xs] = st_new


def _ssd_chain(x, bm, cm, st, dt, dta, first_head, fwd, incl, outs, key):
    t = SSD_CHUNK
    e_heads = SSD_HEADS // SSD_GROUPS
    lane = lax.broadcasted_iota(jnp.int32, (t, 2 * SSD_HEADS), 1)
    lane_x = lax.broadcasted_iota(jnp.int32, (t, SSD_GW), 1) // SSD_HD
    lane_e = lax.broadcasted_iota(jnp.int32, (t, LANES), 1)
    dt_x = jnp.zeros((t, SSD_GW), F32)
    dta4 = jnp.zeros((t, LANES), F32)
    for e in range(e_heads):
        pick = lane == first_head + e
        dt_e = jnp.sum(jnp.where(pick, dt, 0.0), axis=1, keepdims=True)
        dta_e = jnp.sum(jnp.where(pick, dta, 0.0), axis=1, keepdims=True)
        dt_x = jnp.where(lane_x == e, dt_e, dt_x)
        dta4 = jnp.where(lane_e == e, dta_e, dta4)

    cum4 = _mm_sel(incl.astype(BF16), dta4)
    cb = _mm_nt(cm, bm)
    y_off_raw = _mm(cm, st)
    yield
    cum_t = cum4.T
    last4 = jnp.where(fwd, cum4[t - 1:t, :], cum4[0:1, :])
    cum_x = jnp.zeros((t, SSD_GW), F32)
    last_x = jnp.zeros((1, SSD_GW), F32)
    for e in range(e_heads):
        cum_x = jnp.where(lane_x == e, cum4[:, e:e + 1], cum_x)
        last_x = jnp.where(lane_x[:1] == e, last4[:, e:e + 1], last_x)

    xdt = x * dt_x
    zero_h = jnp.zeros((t, LANES), F32)
    lane_h = lax.broadcasted_iota(jnp.int32, (t, LANES), 1) < SSD_HD
    ydiag = []
    for pair in range(e_heads // 2):
        lmats = []
        for e in (2 * pair, 2 * pair + 1):
            seg = cum4[:, e:e + 1] - cum_t[e:e + 1, :]
            lmats.append(cb * jnp.exp(jnp.where(incl, seg, NEG_BIG)))
        xp = xdt[:, pair * LANES:(pair + 1) * LANES]
        xbd = jnp.concatenate([jnp.where(lane_h, xp, zero_h), jnp.where(lane_h, zero_h, xp)], axis=0)
        ydiag.append(_mm(jnp.concatenate(lmats, axis=1), xbd))
    states = _mm_tn(bm, xdt * jnp.exp(last_x - cum_x))
    yield
    y = jnp.concatenate(ydiag, axis=1) + y_off_raw * jnp.exp(cum_x)
    outs[key] = (y, st * jnp.exp(last_x) + states)


def _ssd_scan(xbc, dt_raw, dt_bias, a_log, n_ctx):
    b, n_all, _ = xbc.shape
    t = SSD_CHUNK
    nc = n_all // t
    ncc = n_ctx // t
    cidx = lambda d, i: _chunk_order(d, i, ncc, nc)
    gg = SSD_GROUPS_PER_STEP
    nsteps_g = SSD_GROUPS // gg
    xw, nw = gg * SSD_GW, gg * SSD_STATE
    b0 = SSD_W // nw
    row = pl.BlockSpec((1, 2 * SSD_HEADS), lambda bi, d, g, i: (0, 0))
    return pl.pallas_call(
        _ssd_scan_kernel,
        grid=(b, 2, nsteps_g, nc),
        in_specs=[pl.BlockSpec((1, t, xw), lambda bi, d, g, i: (bi, cidx(d, i), g)),
                  pl.BlockSpec((1, t, nw), lambda bi, d, g, i: (bi, cidx(d, i), b0 + g)),
                  pl.BlockSpec((1, t, nw), lambda bi, d, g, i: (bi, cidx(d, i), b0 + nsteps_g + g)),
                  pl.BlockSpec((1, t, 2 * SSD_HEADS), lambda bi, d, g, i: (bi, cidx(d, i), 0)),
                  row, row],
        out_specs=pl.BlockSpec((1, 1, t, xw), lambda bi, d, g, i: (bi, d, cidx(d, i), g)),
        out_shape=jax.ShapeDtypeStruct((b, 2, n_all, SSD_W), F32),
        scratch_shapes=[pltpu.VMEM((SSD_STATE, xw), F32)],
        compiler_params=pltpu.CompilerParams(dimension_semantics=("arbitrary",) * 4, vmem_limit_bytes=VMEM_LIMIT),
        name="ssd_scan",
    )(xbc, xbc, xbc, dt_raw, dt_bias.reshape(1, -1), a_log.reshape(1, -1))


def _rms_norm(x, g, eps=EPS):
    return x * lax.rsqrt(jnp.mean(x * x, axis=-1, keepdims=True) + eps) * g


HALO = 8


def _dwconv_kernel(n_ctx, n_all, silu, cur_ref, prev_ref, next_ref, w_ref, b_ref, o_ref):
    tr = cur_ref.shape[1]
    taps = w_ref.shape[0]
    pad = taps // 2
    ext = jnp.concatenate([prev_ref[0], cur_ref[0], next_ref[0]], axis=0)
    t = pl.program_id(1) * tr + lax.broadcasted_iota(jnp.int32, (tr, 1), 0)
    acc = cur_ref[0] * w_ref[pad:pad + 1, :]
    for o in range(-pad, pad + 1):
        if o == 0:
            continue
        ts = t + o
        lo, hi = (t, ts) if o > 0 else (ts, t)
        crosses = (lo < n_ctx) & (hi >= n_ctx)
        valid = (ts >= 0) & (ts < n_all) & jnp.logical_not(crosses)
        src = ext[HALO + o:HALO + o + tr]
        acc = acc + jnp.where(valid, src, 0.0) * w_ref[o + pad:o + pad + 1, :]
    if silu:
        acc = acc + b_ref[...]
        acc = acc * _sigmoid(acc)
    o_ref[0] = acc


def _dwconv(u, col0, ncols, w, bias, n_ctx, silu):
    b, n_all, _ = u.shape
    tr = _pick(n_all, (1408, 768, 512, 384, 256, 128))
    wt = _pick(math.gcd(ncols, col0) if col0 else ncols, (512, 256, 128))
    c0 = col0 // wt
    hb = tr // HALO
    last = n_all // HALO - 1
    if bias is None:
        bias = jnp.zeros((ncols,), F32)
    return pl.pallas_call(
        functools.partial(_dwconv_kernel, n_ctx, n_all, silu),
        grid=(b, n_all // tr, ncols // wt),
        in_specs=[pl.BlockSpec((1, tr, wt), lambda bi, i, j: (bi, i, c0 + j)),
                  pl.BlockSpec((1, HALO, wt), lambda bi, i, j: (bi, jnp.maximum(i * hb - 1, 0), c0 + j)),
                  pl.BlockSpec((1, HALO, wt), lambda bi, i, j: (bi, jnp.minimum((i + 1) * hb, last), c0 + j)),
                  pl.BlockSpec((w.shape[0], wt), lambda bi, i, j: (0, j)),
                  pl.BlockSpec((1, wt), lambda bi, i, j: (0, j))],
        out_specs=pl.BlockSpec((1, tr, wt), lambda bi, i, j: (bi, i, j)),
        out_shape=jax.ShapeDtypeStruct((b, n_all, ncols), F32),
        compiler_params=pltpu.CompilerParams(dimension_semantics=("arbitrary",) * 3, vmem_limit_bytes=VMEM_LIMIT),
        name="dwconv",
    )(u, u, u, w, bias.reshape(1, ncols))


def _row_select(tbl_ref, i, tm, rows_per_batch, n_ctx):
    tiles_per_batch = rows_per_batch // tm
    bi = i // tiles_per_batch
    row = (i % tiles_per_batch) * tm + lax.broadcasted_iota(jnp.int32, (tm, 1), 0)
    nb = tbl_ref.shape[0] - 1
    return jnp.where(row < n_ctx, tbl_ref[nb:nb + 1, :], tbl_ref[pl.ds(bi, 1), :])


def _prenorm_kernel(n_ctx, rows_per_batch, x_ref, g_ref, shift_ref, scale_ref, o_ref):
    i = pl.program_id(0)
    tm = x_ref.shape[0]
    x = x_ref[...]
    y = x * lax.rsqrt(jnp.mean(x * x, axis=-1, keepdims=True) + EPS) * g_ref[...]
    scale = _row_select(scale_ref, i, tm, rows_per_batch, n_ctx)
    shift = _row_select(shift_ref, i, tm, rows_per_batch, n_ctx)
    o_ref[...] = (y * (1.0 + scale) + shift).astype(o_ref.dtype)


def _prenorm(x2d, g, shift_tbl, scale_tbl, n_ctx, rows_per_batch):
    m, n = x2d.shape
    tm = _pick(rows_per_batch, (384, 256, 128))
    tbl = pl.BlockSpec(shift_tbl.shape, lambda i: (0, 0))
    return pl.pallas_call(
        functools.partial(_prenorm_kernel, n_ctx, rows_per_batch),
        grid=(m // tm,),
        in_specs=[pl.BlockSpec((tm, n), lambda i: (i, 0)), pl.BlockSpec((1, n), lambda i: (0, 0)), tbl, tbl],
        out_specs=pl.BlockSpec((tm, n), lambda i: (i, 0)),
        out_shape=jax.ShapeDtypeStruct((m, n), BF16),
        compiler_params=pltpu.CompilerParams(dimension_semantics=("arbitrary",), vmem_limit_bytes=VMEM_LIMIT),
        name="prenorm",
    )(x2d, g.reshape(1, n), shift_tbl, scale_tbl)


def _rwkv_finish_kernel(yf_ref, yb_ref, bonus_ref, g_ref, w_ref, b_ref, o_ref):
    y = yf_ref[0] + yb_ref[0]
    hr = lax.broadcasted_iota(jnp.int32, (LANES, LANES), 0) // RWKV_HD
    hc = lax.broadcasted_iota(jnp.int32, (LANES, LANES), 1) // RWKV_HD
    head_ones = (hr == hc).astype(BF16)
    outs = []
    for p in range(y.shape[1] // LANES):
        yp = y[:, p * LANES:(p + 1) * LANES]
        mu = _mm_sel_right(yp, head_ones) * (1.0 / RWKV_HD)
        dev = yp - mu
        var = _mm_sel_right(dev * dev, head_ones) * (1.0 / RWKV_HD)
        outs.append(dev * lax.rsqrt(var + RWKV_GN_EPS))
    yn = jnp.concatenate(outs, axis=1) * w_ref[...] + b_ref[...]
    g = g_ref[0]
    o_ref[0] = ((yn + bonus_ref[0]) * (g * _sigmoid(g))).astype(o_ref.dtype)


def _rwkv_finish(yf, yb, bonus, ua, g_col0, lnx_w, lnx_b):
    b, n_all, _ = yf.shape
    tr = _pick(n_all, (768, 512, 384, 256, 128))
    wt = 256
    g0 = g_col0 // wt
    vec = pl.BlockSpec((1, wt), lambda bi, i, j: (0, j))
    blk = lambda c0: pl.BlockSpec((1, tr, wt), lambda bi, i, j: (bi, i, c0 + j))
    return pl.pallas_call(
        _rwkv_finish_kernel,
        grid=(b, n_all // tr, RWKV_W // wt),
        in_specs=[blk(0), blk(0), blk(0), blk(g0), vec, vec],
        out_specs=blk(0),
        out_shape=jax.ShapeDtypeStruct((b, n_all, RWKV_W), BF16),
        compiler_params=pltpu.CompilerParams(dimension_semantics=("arbitrary",) * 3, vmem_limit_bytes=VMEM_LIMIT),
        name="rwkv_finish",
    )(yf, yb, bonus, ua, lnx_w.reshape(1, -1), lnx_b.reshape(1, -1))


def _ssd_finish_kernel(y_ref, x_ref, z_ref, d_ref, g_ref, o_ref):
    z = z_ref[0]
    y = (y_ref[0, 0] + y_ref[0, 1] + d_ref[...] * x_ref[0]) * (z * _sigmoid(z))
    outs = []
    for grp in range(y.shape[1] // SSD_GW):
        yg = y[:, grp * SSD_GW:(grp + 1) * SSD_GW]
        outs.append(yg * lax.rsqrt(jnp.mean(yg * yg, axis=-1, keepdims=True) + EPS))
    o_ref[0] = (jnp.concatenate(outs, axis=1) * g_ref[...]).astype(o_ref.dtype)


def _ssd_finish(y2, xact, uc, d_vec, norm_g):
    b, _, n_all, _ = y2.shape
    tr = _pick(n_all, (384, 256, 128))
    wt = 1024
    vec = pl.BlockSpec((1, wt), lambda bi, i, j: (0, j))
    blk = pl.BlockSpec((1, tr, wt), lambda bi, i, j: (bi, i, j))
    return pl.pallas_call(
        _ssd_finish_kernel,
        grid=(b, n_all // tr, SSD_W // wt),
        in_specs=[pl.BlockSpec((1, 2, tr, wt), lambda bi, i, j: (bi, 0, i, j)), blk, blk, vec, vec],
        out_specs=blk,
        out_shape=jax.ShapeDtypeStruct((b, n_all, SSD_W), BF16),
        compiler_params=pltpu.CompilerParams(dimension_semantics=("arbitrary",) * 3, vmem_limit_bytes=VMEM_LIMIT),
        name="ssd_finish",
    )(y2, xact, uc, d_vec.reshape(1, -1), norm_g.reshape(1, -1))


def _axial_rope(n_tok):
    rows = n_tok // GRID_W
    row = jnp.repeat(jnp.arange(rows, dtype=F32), GRID_W)
    col = jnp.tile(jnp.arange(GRID_W, dtype=F32), rows)
    n_freq = DIFF_HD // 4
    inv = ROPE_BASE ** (-jnp.arange(n_freq, dtype=F32) / n_freq)
    ang = jnp.concatenate([row[:, None] * inv, col[:, None] * inv], axis=-1)
    return jnp.cos(ang), jnp.sin(ang)


def _apply_rope(t, cos, sin):
    b, n, _ = t.shape
    t5 = t.reshape(b, n, DIFF_HEADS, 2, DIFF_HD)
    half = DIFF_HD // 2
    c = cos[None, :, None, None, :]
    s = sin[None, :, None, None, :]
    t1, t2 = t5[..., :half], t5[..., half:]
    return jnp.concatenate([t1 * c - t2 * s, t2 * c + t1 * s], axis=-1).reshape(b, n, DIFF_W)


def _rwkv_branch(ua, n_ctx, conv_w, w0, w2, a0, a2, k_k, k_a, r_k, lnx_w, lnx_b):
    uconv = _dwconv(ua, 0, RWKV_CONV_CH, conv_w, None, n_ctx, False)
    m, n, q, yl, bonus = _rwkv_local(uconv, w0, w2, a0, a2, k_k, k_a, r_k)
    yf, yb = _rwkv_scan(m, n, q, yl, n_ctx)
    return _rwkv_finish(yf, yb, bonus, ua, RWKV_CONV_CH, lnx_w, lnx_b)


def _diff_branch(q, k, v, g, n_ctx, cos, sin, lam_p, subln_g, lam_init):
    b, n_all, _ = q.shape
    lam = jnp.exp(jnp.sum(lam_p[0] * lam_p[1])) - jnp.exp(jnp.sum(lam_p[2] * lam_p[3])) + lam_init
    scale = DIFF_HD ** -0.5 * math.log2(math.e)
    qc, ql = q[:, :n_ctx], _apply_rope(q[:, n_ctx:], cos, sin)
    kc, kl = k[:, :n_ctx], _apply_rope(k[:, n_ctx:], cos, sin)
    k_all = jnp.concatenate([kc, kl], axis=1).astype(BF16)
    vt_all = jnp.swapaxes(v, 1, 2).astype(BF16)
    feat_major = lambda t: jnp.swapaxes(t * scale, 1, 2).astype(BF16)
    oc = _diff_attn(feat_major(qc), k_all[:, :n_ctx], vt_all[:, :, :n_ctx], lam)
    ol = _diff_attn(feat_major(ql), k_all, vt_all, lam)
    o = jnp.swapaxes(jnp.concatenate([oc, ol], axis=2), 1, 2).reshape(b, n_all, DIFF_HEADS, DIFF_VD)
    o = _rms_norm(o, subln_g) * (1.0 - lam_init)
    return o.reshape(b, n_all, DIFF_W) * jax.nn.silu(g)


def _ssd_branch(uc, dt_raw, n_ctx, conv_w, conv_b, a_log, dt_bias, d_skip, norm_g):
    xact = _dwconv(uc, SSD_W, SSD_XBC, conv_w, conv_b, n_ctx, True)
    y2 = _ssd_scan(xact, dt_raw, dt_bias, a_log, n_ctx)
    return _ssd_finish(y2, xact, uc, jnp.repeat(d_skip, SSD_HD), norm_g)


def kernel(x, c, ctx, c_ctx, w_mod, b_mod, g_pre, g_post, w_in, rwkv_conv, rwkv_w0, rwkv_w2, rwkv_a0, rwkv_a2,
           rwkv_k_k, rwkv_k_a, rwkv_r_k, rwkv_lnx_w, rwkv_lnx_b, diff_lambda, diff_subln, ssd_conv_w, ssd_conv_b,
           ssd_a_log, ssd_dt_bias, ssd_d, ssd_norm, w_branch_a, w_branch_b, w_branch_c, w_out):
    b, n_lat, dm = x.shape
    n_ctx = ctx.shape[1]
    n_all = n_ctx + n_lat
    depth = w_in.shape[0]
    cos, sin = _axial_rope(n_lat)
    cond = jax.nn.silu(jnp.concatenate([c, c_ctx[None, :]], axis=0))
    cond = jnp.pad(cond, ((0, 8 - (b + 1)), (0, 0)))

    o_rw, o_g, o_dt = 0, RWKV_CONV_CH, RWKV_CONV_CH + RWKV_W
    src_dt = o_dt + 4 * DIFF_W + SSD_W + SSD_XBC
    pad_a = -(o_dt + 2 * SSD_HEADS) % 512

    xa = jnp.concatenate([ctx, x], axis=1)
    for li in range(depth):
        lam_init = 0.8 - 0.6 * math.exp(-0.3 * li)
        mod = _matmul(cond, w_mod, tn_cands=(1024, 512, 256, 128), layer=li)[:b + 1] + b_mod[li]
        shift_l, scale_l, gate_l = jnp.split(mod[:b], 3, axis=-1)
        shift_c, scale_c, gate_c = jnp.split(mod[b], 3, axis=-1)
        tbl = lambda vl, vc: jnp.concatenate([vl, vc[None, :]], axis=0)
        hb = _prenorm(xa.reshape(b * n_all, dm), g_pre[li], tbl(shift_l, shift_c), tbl(scale_l, scale_c), n_ctx,
                      n_all)
        wl = w_in[li]
        proj = lambda w: _matmul(hb, w.astype(BF16), tn_cands=(1024, 512, 256, 128)).reshape(b, n_all, -1)
        ua = proj(jnp.concatenate([wl[:, :o_dt], wl[:, src_dt:src_dt + 2 * SSD_HEADS],
                                   jnp.zeros((dm, pad_a), wl.dtype)], axis=1))
        ub = proj(wl[:, o_dt:o_dt + 4 * DIFF_W])
        uc = proj(wl[:, o_dt + 4 * DIFF_W:src_dt])
        ug = proj(wl[:, src_dt + 2 * SSD_HEADS:])

        oa = _rwkv_branch(ua, n_ctx, rwkv_conv[li], rwkv_w0[li], rwkv_w2[li], rwkv_a0[li], rwkv_a2[li],
                          rwkv_k_k[li], rwkv_k_a[li], rwkv_r_k[li], rwkv_lnx_w[li], rwkv_lnx_b[li])
        ob = _diff_branch(ub[..., :DIFF_W], ub[..., DIFF_W:2 * DIFF_W], ub[..., 2 * DIFF_W:3 * DIFF_W],
                          ub[..., 3 * DIFF_W:], n_ctx, cos, sin, diff_lambda[li], diff_subln[li], lam_init)
        oc = _ssd_branch(uc, ua[..., o_dt:o_dt + 2 * SSD_HEADS], n_ctx, ssd_conv_w[li], ssd_conv_b[li],
                         ssd_a_log[li], ssd_dt_bias[li], ssd_d[li], ssd_norm[li])

        flat = lambda t: t.astype(BF16).reshape(b * n_all, -1)
        mrg = _merge(flat(oa), flat(ob), flat(oc), w_branch_a[li].astype(BF16), w_branch_b[li].astype(BF16),
                     w_branch_c[li].astype(BF16), ug.reshape(b * n_all, 3 * dm))
        xa = _out_norm(mrg, w_out[li].astype(BF16), xa.reshape(b * n_all, dm), tbl(gate_l, gate_c), g_post[li],
                       n_ctx, n_all).reshape(b, n_all, dm)
    return xa[:, n_ctx:]
```

```python
import functools
import math

import jax
import jax.numpy as jnp
from jax import lax
from jax.experimental import pallas as pl
from jax.experimental.pallas import tpu as pltpu

F32 = jnp.float32
BF16 = jnp.bfloat16

EPS = 1e-6
GRID_W = 64
ROPE_BASE = 10000.0

RWKV_HEADS = 16
RWKV_HD = 64
RWKV_W = RWKV_HEADS * RWKV_HD
RWKV_LORA = 64
RWKV_CONV_CH = 3 * RWKV_W + 4 * RWKV_LORA
RWKV_GN_EPS = 64e-5
RWKV_CHUNK = 64
RWKV_LOCAL_WIDTH = 1024

DIFF_HEADS = 8
DIFF_HD = 64
DIFF_VD = 2 * DIFF_HD
DIFF_W = DIFF_HEADS * DIFF_VD

SSD_HEADS = 32
SSD_HD = 64
SSD_W = SSD_HEADS * SSD_HD
SSD_GROUPS = 8
SSD_STATE = 128
SSD_CHUNK = 128
SSD_XBC = SSD_W + 2 * SSD_GROUPS * SSD_STATE
SSD_GW = SSD_W // SSD_GROUPS

LANES = 128
VMEM_LIMIT = 56 * 1024 * 1024

NEG_BIG = -1e30


def _pick(n, candidates):
    for c in candidates:
        if n % c == 0:
            return c
    return n


def _mm(a, b):
    return jnp.dot(a.astype(BF16), b.astype(BF16), preferred_element_type=F32)


def _mm_nt(a, b):
    return lax.dot_general(a.astype(BF16), b.astype(BF16), (((1,), (1,)), ((), ())), preferred_element_type=F32)


def _mm_tn(a, b):
    return lax.dot_general(a.astype(BF16), b.astype(BF16), (((0,), (0,)), ((), ())), preferred_element_type=F32)


def _split3(x):
    x1 = x.astype(BF16)
    r1 = x - x1.astype(F32)
    x2 = r1.astype(BF16)
    x3 = (r1 - x2.astype(F32)).astype(BF16)
    return x1, x2, x3


def _mm_sel(sel, x):
    x1, x2, x3 = _split3(x)
    d = lambda y: jnp.dot(sel, y, preferred_element_type=F32)
    return d(x1) + d(x2) + d(x3)


def _softplus(x):
    return jnp.maximum(x, 0.0) + jnp.log(1.0 + jnp.exp(-jnp.abs(x)))


def _sigmoid(x):
    return 1.0 / (1.0 + jnp.exp(-x))


def _matmul_kernel(a_ref, w_ref, o_ref):
    o_ref[...] = _mm(a_ref[...], w_ref[...]).astype(o_ref.dtype)


def _matmul(a, w, out_dtype=F32, tm_cands=(768, 512, 384, 256, 128, 8), tn_cands=(512, 256, 128), layer=None):
    m, k = a.shape
    n = w.shape[-1]
    tm = _pick(m, tm_cands)
    tn = _pick(n, tn_cands)
    if layer is None:
        w_spec = pl.BlockSpec((k, tn), lambda i, j: (0, j))
    else:
        w_spec = pl.BlockSpec((None, k, tn), lambda i, j: (layer, 0, j))
    return pl.pallas_call(
        _matmul_kernel,
        grid=(m // tm, n // tn),
        in_specs=[pl.BlockSpec((tm, k), lambda i, j: (i, 0)), w_spec],
        out_specs=pl.BlockSpec((tm, tn), lambda i, j: (i, j)),
        out_shape=jax.ShapeDtypeStruct((m, n), out_dtype),
        compiler_params=pltpu.CompilerParams(dimension_semantics=("arbitrary", "arbitrary"),
                                             vmem_limit_bytes=VMEM_LIMIT),
        name="dense_matmul",
    )(a, w)


def _merge_kernel(oa_ref, ob_ref, oc_ref, wa_ref, wb_ref, wc_ref, sa_ref, sb_ref, sc_ref, o_ref):
    dot = lambda x_ref, w_ref: jnp.dot(x_ref[...], w_ref[...], preferred_element_type=F32)
    o = (_sigmoid(sa_ref[...]) * dot(oa_ref, wa_ref) + _sigmoid(sb_ref[...]) * dot(ob_ref, wb_ref)
         + _sigmoid(sc_ref[...]) * dot(oc_ref, wc_ref))
    o_ref[...] = o.astype(o_ref.dtype)


def _merge(oa, ob, oc, wa, wb, wc, gates):
    m = oa.shape[0]
    n = wa.shape[1]
    tm = _pick(m, (768, 512, 384, 256, 128))
    tn = _pick(n, (512, 256, 128))
    nj = n // tn
    act = lambda t: pl.BlockSpec((tm, t.shape[1]), lambda i, j: (i, 0))
    wgt = lambda t: pl.BlockSpec((t.shape[0], tn), lambda i, j: (0, j))
    gate = lambda s: pl.BlockSpec((tm, tn), lambda i, j: (i, s * nj + j))
    return pl.pallas_call(
        _merge_kernel,
        grid=(m // tm, nj),
        in_specs=[act(oa), act(ob), act(oc), wgt(wa), wgt(wb), wgt(wc), gate(0), gate(1), gate(2)],
        out_specs=pl.BlockSpec((tm, tn), lambda i, j: (i, j)),
        out_shape=jax.ShapeDtypeStruct((m, n), BF16),
        compiler_params=pltpu.CompilerParams(dimension_semantics=("arbitrary", "arbitrary"),
                                             vmem_limit_bytes=VMEM_LIMIT),
        name="branch_merge",
    )(oa, ob, oc, wa, wb, wc, gates, gates, gates)


def _out_norm_kernel(n_ctx, rows_per_batch, m_ref, w_ref, x_ref, gate_ref, g_ref, o_ref, acc_ref):
    i = pl.program_id(0)
    kk = pl.program_id(1)

    @pl.when(kk == 0)
    def _():
        acc_ref[...] = jnp.zeros_like(acc_ref)

    acc_ref[...] += jnp.dot(m_ref[...], w_ref[...], preferred_element_type=F32)

    @pl.when(kk == pl.num_programs(1) - 1)
    def _():
        tm = acc_ref.shape[0]
        out = acc_ref[...]
        y = out * lax.rsqrt(jnp.mean(out * out, axis=-1, keepdims=True) + EPS) * g_ref[...]
        tiles_per_batch = rows_per_batch // tm
        bi = i // tiles_per_batch
        row = (i % tiles_per_batch) * tm + lax.broadcasted_iota(jnp.int32, (tm, 1), 0)
        nb = gate_ref.shape[0] - 1
        gate = jnp.where(row < n_ctx, gate_ref[nb:nb + 1, :], gate_ref[pl.ds(bi, 1), :])
        o_ref[...] = x_ref[...] + gate * y


def _out_norm(mrg, w, x2d, gates, g_post, n_ctx, rows_per_batch):
    m, k = mrg.shape
    n = w.shape[1]
    tm = _pick(rows_per_batch, (384, 256, 128))
    tk = _pick(k, (512, 256, 128))
    return pl.pallas_call(
        functools.partial(_out_norm_kernel, n_ctx, rows_per_batch),
        grid=(m // tm, k // tk),
        in_specs=[pl.BlockSpec((tm, tk), lambda i, kk: (i, kk)),
                  pl.BlockSpec((tk, n), lambda i, kk: (kk, 0)),
                  pl.BlockSpec((tm, n), lambda i, kk: (i, 0)),
                  pl.BlockSpec(gates.shape, lambda i, kk: (0, 0)),
                  pl.BlockSpec((1, n), lambda i, kk: (0, 0))],
        out_specs=pl.BlockSpec((tm, n), lambda i, kk: (i, 0)),
        out_shape=jax.ShapeDtypeStruct((m, n), F32),
        scratch_shapes=[pltpu.VMEM((tm, n), F32)],
        compiler_params=pltpu.CompilerParams(dimension_semantics=("arbitrary", "arbitrary"),
                                             vmem_limit_bytes=VMEM_LIMIT),
        name="out_norm",
    )(mrg, w, x2d, gates, g_post.reshape(1, n))


def _head_masks(rows):
    lane = lax.broadcasted_iota(jnp.int32, (rows, LANES), 1)
    return lane < RWKV_HD


def _bd(x, first):
    zero = jnp.zeros_like(x)
    return jnp.concatenate([jnp.where(first, x, zero), jnp.where(first, zero, x)], axis=0)


def _rwkv_local_kernel(r_ref, k_ref, v_ref, lw_ref, la_ref, w0_ref, w2_ref, a0_ref, a2_ref, kk_ref, ka_ref,
                       rk_ref, m_ref, n_ref, q_ref, yl_ref, bonus_ref):
    c = RWKV_CHUNK
    width = r_ref.shape[2]
    first = _head_masks(c)
    row = lax.broadcasted_iota(jnp.int32, (c, LANES), 0)
    col = lax.broadcasted_iota(jnp.int32, (c, LANES), 1) % RWKV_HD
    eye_cat = (row == col).astype(F32)
    hr = lax.broadcasted_iota(jnp.int32, (LANES, LANES), 0) // RWKV_HD
    hc = lax.broadcasted_iota(jnp.int32, (LANES, LANES), 1) // RWKV_HD
    head_ones = (hr == hc).astype(BF16)
    tr = lax.broadcasted_iota(jnp.int32, (c, c), 0)
    tc = lax.broadcasted_iota(jnp.int32, (c, c), 1)
    zrow = jnp.zeros((RWKV_LORA, width), F32)

    tanh_lw = jnp.tanh(lw_ref[0])
    la = la_ref[0]
    w_raw2, a_sig2 = [], []
    for d in (0, 1):
        pad = (lambda m: jnp.concatenate([m, zrow], axis=0)) if d == 0 else (lambda m: jnp.concatenate([zrow, m], axis=0))
        w_raw2.append(w0_ref[d:d + 1, :] + _mm(tanh_lw, pad(w2_ref[d])))
        a_sig2.append(_sigmoid(a0_ref[d:d + 1, :] + _mm(la, pad(a2_ref[d]))))

    r_all, k_all, v_all = r_ref[0], k_ref[0], v_ref[0]
    kk_all, ka_all, rk_all = kk_ref[...], ka_ref[...], rk_ref[...]
    consts = (first, row, col, eye_cat, tr, tc)
    chains, outs = [], {}
    for pi in range(width // LANES):
        sl = slice(pi * LANES, (pi + 1) * LANES)
        r, k, v = r_all[:, sl], k_all[:, sl], v_all[:, sl]
        kkp = k * kk_all[:, sl]
        kk = kkp / jnp.maximum(jnp.sqrt(_mm_sel_right(kkp * kkp, head_ones)), 1e-12)
        a_sum = a_sig2[0][:, sl] + a_sig2[1][:, sl]
        kd_sum = k * (2.0 + (a_sum - 2.0) * ka_all[:, sl])
        outs[pi] = _mm_sel_right(r * kd_sum * rk_all[:, sl], head_ones) * v
        for d in (0, 1):
            chains.append(_rwkv_chain(d, r, k, v, kk, w_raw2[d][:, sl], a_sig2[d][:, sl], ka_all[:, sl], consts,
                                      outs, (pi, d)))
    while chains:
        alive = []
        for ch in chains:
            if next(ch, _DONE) is not _DONE:
                alive.append(ch)
        chains = alive
    for pi in range(width // LANES):
        sl = slice(pi * LANES, (pi + 1) * LANES)
        for d in (0, 1):
            m_cat, n_cat, q, yloc = outs[(pi, d)]
            m_ref[0, d, 0, :, sl] = m_cat
            n_ref[0, d, 0, :, sl] = n_cat
            q_ref[0, d, :, sl] = q
            yl_ref[0, d, :, sl] = yloc
        bonus_ref[0, :, sl] = outs[pi]


_DONE = object()


def _rwkv_chain(d, r, k, v, kk, w_raw, a_sig, ka, consts, outs, key):
    c = RWKV_CHUNK
    first, row, col, eye_cat, tr, tc = consts
    if d == 0:
        incl, strict, tri = col <= row, col < row, (tc <= tr)
    else:
        incl, strict, tri = col >= row, col > row, (tc >= tr)
    logw = -jnp.exp(-_softplus(-w_raw) - 0.5)
    k_dir = k * (1.0 + (a_sig - 1.0) * ka)
    b_dir = kk * a_sig

    cum = _mm_sel(tri.astype(BF16), logw)
    yield
    tot = cum[c - 1:c, :] if d == 0 else cum[0:1, :]
    g_incl = jnp.exp(cum)
    g_excl = jnp.exp(cum - logw)
    g_inv = jnp.exp(-cum)
    g_hat = jnp.exp(tot - cum)
    at = -kk * g_excl
    rt = r * g_incl
    bt = b_dir * g_inv
    kt = k_dir * g_inv
    bh = b_dir * g_hat
    kh = k_dir * g_hat

    sc = _mm_nt(jnp.concatenate([at, rt], axis=0),
                jnp.concatenate([_bd(bt, first), _bd(kt, first)], axis=0))
    yield
    zero = jnp.zeros((c, LANES), F32)
    a_ab = jnp.where(strict, sc[:c, :LANES], zero)
    a_ak = jnp.where(strict, sc[:c, LANES:], zero)
    a_rb = jnp.where(incl, sc[c:, :LANES], zero)
    a_rk = jnp.where(incl, sc[c:, LANES:], zero)

    t = eye_cat + a_ab
    p = _mm(a_ab, _bd(a_ab, first))
    akv = _mm(a_ak, _bd(v, first))
    yield
    for _ in range(4):
        pbd = _bd(p, first)
        t, p = t + _mm(t, pbd), _mm(p, pbd)
        yield
    t = t + _mm(t, _bd(p, first))
    yield

    x = _mm(t, jnp.concatenate([_bd(akv, first), _bd(at, first)], axis=1))
    yield
    uloc = x[:, :LANES]
    w = x[:, LANES:]
    yloc = _mm(jnp.concatenate([a_rb, a_rk], axis=1),
               jnp.concatenate([_bd(uloc, first), _bd(v, first)], axis=0))
    q = rt + _mm(a_rb, _bd(w, first))
    g = _mm_tn(jnp.concatenate([bh, kh], axis=1),
               jnp.concatenate([uloc, w, v], axis=1))
    yield
    nfull = g[:LANES, :LANES] + g[LANES:, 2 * LANES:]
    mfull = g[:LANES, LANES:2 * LANES]
    n_cat = jnp.where(first, nfull[:c], zero) + jnp.where(first, zero, nfull[c:])
    m_cat = jnp.where(first, mfull[:c], zero) + jnp.where(first, zero, mfull[c:]) + eye_cat * jnp.exp(tot)
    outs[key] = (m_cat, n_cat, q, yloc)


def _mm_sel_right(x, sel):
    x1, x2, x3 = _split3(x)
    d = lambda y: jnp.dot(y, sel, preferred_element_type=F32)
    return d(x1) + d(x2) + d(x3)


def _rwkv_local(uconv, w0, w2, a0, a2, k_k, k_a, r_k):
    b, n_all, _ = uconv.shape
    c = RWKV_CHUNK
    nc = n_all // c
    wb = RWKV_LOCAL_WIDTH
    nblk = RWKV_W // wb
    row = lambda t: t.reshape(1, RWKV_W)
    col_blk = lambda off: pl.BlockSpec((1, c, wb), lambda bi, ci, pi: (bi, ci, off + pi))
    lora_blk = lambda off: pl.BlockSpec((1, c, LANES), lambda bi, ci, pi: (bi, ci, off))
    vec2 = pl.BlockSpec((2, wb), lambda bi, ci, pi: (0, pi))
    mat2 = pl.BlockSpec((2, RWKV_LORA, wb), lambda bi, ci, pi: (0, 0, pi))
    vec1 = pl.BlockSpec((1, wb), lambda bi, ci, pi: (0, pi))
    mn_shape = jax.ShapeDtypeStruct((b, 2, nc, RWKV_HD, RWKV_W), F32)
    mn_spec = pl.BlockSpec((1, 2, 1, RWKV_HD, wb), lambda bi, ci, pi: (bi, 0, ci, 0, pi))
    qy_shape = jax.ShapeDtypeStruct((b, 2, n_all, RWKV_W), F32)
    qy_spec = pl.BlockSpec((1, 2, c, wb), lambda bi, ci, pi: (bi, 0, ci, pi))
    lora0 = 3 * RWKV_W // LANES
    return pl.pallas_call(
        _rwkv_local_kernel,
        grid=(b, nc, nblk),
        in_specs=[col_blk(0), col_blk(nblk), col_blk(2 * nblk), lora_blk(lora0), lora_blk(lora0 + 1),
                  vec2, mat2, vec2, mat2, vec1, vec1, vec1],
        out_specs=[mn_spec, mn_spec, qy_spec, qy_spec,
                   pl.BlockSpec((1, c, wb), lambda bi, ci, pi: (bi, ci, pi))],
        out_shape=[mn_shape, mn_shape, qy_shape, qy_shape, jax.ShapeDtypeStruct((b, n_all, RWKV_W), F32)],
        compiler_params=pltpu.CompilerParams(dimension_semantics=("arbitrary",) * 3, vmem_limit_bytes=VMEM_LIMIT),
        name="rwkv_local",
    )(uconv, uconv, uconv, uconv, uconv, w0, w2, a0, a2, row(k_k), row(k_a), row(r_k))


def _rwkv_scan_kernel(mf_ref, nf_ref, qf_ref, ylf_ref, mb_ref, nb_ref, qb_ref, ylb_ref, yf_ref, yb_ref, s_ref):
    c = RWKV_CHUNK

    @pl.when(pl.program_id(1) == 0)
    def _():
        s_ref[...] = jnp.zeros_like(s_ref)

    first = _head_masks(RWKV_HD)
    dot = lambda x, y: jnp.dot(x, y, preferred_element_type=F32)
    for d, (m_ref, n_ref, q_ref, yl_ref, y_ref) in enumerate(((mf_ref, nf_ref, qf_ref, ylf_ref, yf_ref),
                                                             (mb_ref, nb_ref, qb_ref, ylb_ref, yb_ref))):
        for p in range(RWKV_W // LANES):
            sl = slice(p * LANES, (p + 1) * LANES)
            s1, s2, s3 = _split3(_bd(s_ref[d, :, sl], first))
            lhs = jnp.concatenate([q_ref[0, 0, :, sl], m_ref[0, 0, 0, :, sl]], axis=0)
            l1, l2, l3 = _split3(lhs)
            z = dot(l1, s1) + (dot(l1, s2) + dot(l2, s1)) + (dot(l2, s2) + dot(l1, s3) + dot(l3, s1))
            y_ref[0, :, sl] = yl_ref[0, 0, :, sl] + z[:c]
            s_ref[d, :, sl] = z[c:] + n_ref[0, 0, 0, :, sl]


def _chunk_order(d, i, n_ctx_chunks, n_chunks):
    back = jnp.where(i < n_ctx_chunks, n_ctx_chunks - 1 - i, n_chunks + n_ctx_chunks - 1 - i)
    return jnp.where(d == 0, i, back)


def _rwkv_scan(m, n, q, yl, n_ctx):
    b, _, nc, _, _ = m.shape
    n_all = q.shape[2]
    c = RWKV_CHUNK
    ncc = n_ctx // c
    cidx = lambda d, i: _chunk_order(d, i, ncc, nc)
    mn_spec = lambda d: pl.BlockSpec((1, 1, 1, RWKV_HD, RWKV_W), lambda bi, i: (bi, d, cidx(d, i), 0, 0))
    qy_spec = lambda d: pl.BlockSpec((1, 1, c, RWKV_W), lambda bi, i: (bi, d, cidx(d, i), 0))
    y_spec = lambda d: pl.BlockSpec((1, c, RWKV_W), lambda bi, i: (bi, cidx(d, i), 0))
    y_shape = jax.ShapeDtypeStruct((b, n_all, RWKV_W), F32)
    return pl.pallas_call(
        _rwkv_scan_kernel,
        grid=(b, nc),
        in_specs=[mn_spec(0), mn_spec(0), qy_spec(0), qy_spec(0), mn_spec(1), mn_spec(1), qy_spec(1), qy_spec(1)],
        out_specs=[y_spec(0), y_spec(1)],
        out_shape=[y_shape, y_shape],
        scratch_shapes=[pltpu.VMEM((2, RWKV_HD, RWKV_W), F32)],
        compiler_params=pltpu.CompilerParams(dimension_semantics=("arbitrary",) * 2, vmem_limit_bytes=VMEM_LIMIT),
        name="rwkv_scan",
    )(m, n, q, yl, m, n, q, yl)


ATTN_COLS = 256


def _diff_attn_kernel(tsub, lam_ref, qt_ref, k_ref, vt_ref, g_ref, subln_ref, o_ref, qs_ref, m_ref, l_ref, acc_ref):
    tq = qt_ref.shape[2]
    tk = k_ref.shape[1]
    kv = pl.program_id(3)

    @pl.when(kv == 0)
    def _():
        qt = qt_ref[0]
        feat = lax.broadcasted_iota(jnp.int32, qt.shape, 0)
        zero = jnp.zeros_like(qt)
        qs_ref[...] = jnp.concatenate([jnp.where(feat < DIFF_HD, qt, zero), jnp.where(feat < DIFF_HD, zero, qt)], axis=1)
        m_ref[...] = jnp.full_like(m_ref, NEG_BIG)
        l_ref[...] = jnp.zeros_like(l_ref)
        acc_ref[...] = jnp.zeros_like(acc_ref)

    nsub = tk // tsub
    strips = [slice(cb * ATTN_COLS, (cb + 1) * ATTN_COLS) for cb in range(2 * tq // ATTN_COLS)]
    ones = jnp.ones((ATTN_ONES_ROWS, tsub), BF16)

    def scores_of(j):
        kj = k_ref[0, j * tsub:(j + 1) * tsub, :]
        return [jnp.dot(kj, qs_ref[:, sl], preferred_element_type=F32) for sl in strips]

    nxt = scores_of(0)
    for j in range(nsub):
        scores = nxt
        if j + 1 < nsub:
            nxt = scores_of(j + 1)
        v_ext = jnp.concatenate([vt_ref[0, :, j * tsub:(j + 1) * tsub], ones], axis=0)
        for sl, s in zip(strips, scores):
            m_old = m_ref[:, sl]
            m_new = jnp.maximum(m_old, jnp.max(s, axis=0, keepdims=True))
            alpha = jnp.exp2(m_old - m_new)
            p = jnp.exp2(s - m_new[:1]).astype(BF16)
            pv = jnp.dot(v_ext, p, preferred_element_type=F32)
            acc_ref[:, sl] = alpha[:1] * acc_ref[:, sl] + pv[:DIFF_VD]
            l_ref[:, sl] = alpha * l_ref[:, sl] + pv[DIFF_VD:DIFF_VD + 8]
            m_ref[:, sl] = m_new

    @pl.when(kv == pl.num_programs(3) - 1)
    def _():
        o = acc_ref[...] / l_ref[:1, :]
        diff = o[:, :tq] - lam_ref[0, 0] * o[:, tq:]
        inv = lax.rsqrt(jnp.mean(diff * diff, axis=0, keepdims=True) + EPS)
        y = (diff * inv * (subln_ref[...] * lam_ref[0, 1])).T
        g = g_ref[0]
        o_ref[0] = (y * (g * _sigmoid(g))).astype(o_ref.dtype)


ATTN_ONES_ROWS = 16
ATTN_MAX_SUBTILES = 11


def _diff_attn(qt, k, vt, g, subln_g, lam, post_scale):
    b, _, nq = qt.shape
    nk = k.shape[1]
    tq = _pick(nq, (512, 256, 128))
    tsub = _pick(nk, (768, 512, 384, 256, 128))
    n_sub = nk // tsub
    tk = tsub * max(s for s in range(1, ATTN_MAX_SUBTILES + 1) if n_sub % s == 0)
    return pl.pallas_call(
        functools.partial(_diff_attn_kernel, tsub),
        grid=(b, DIFF_HEADS, nq // tq, nk // tk),
        in_specs=[pl.BlockSpec(memory_space=pltpu.SMEM),
                  pl.BlockSpec((1, LANES, tq), lambda bi, h, i, j: (bi, h, i)),
                  pl.BlockSpec((1, tk, LANES), lambda bi, h, i, j: (bi, j, h)),
                  pl.BlockSpec((1, LANES, tk), lambda bi, h, i, j: (bi, h, j)),
                  pl.BlockSpec((1, tq, LANES), lambda bi, h, i, j: (bi, i, h)),
                  pl.BlockSpec((DIFF_VD, 1), lambda bi, h, i, j: (0, 0))],
        out_specs=pl.BlockSpec((1, tq, LANES), lambda bi, h, i, j: (bi, i, h)),
        out_shape=jax.ShapeDtypeStruct((b, nq, DIFF_W), BF16),
        scratch_shapes=[pltpu.VMEM((LANES, 2 * tq), BF16), pltpu.VMEM((8, 2 * tq), F32),
                        pltpu.VMEM((8, 2 * tq), F32), pltpu.VMEM((DIFF_VD, 2 * tq), F32)],
        compiler_params=pltpu.CompilerParams(dimension_semantics=("arbitrary",) * 4, vmem_limit_bytes=VMEM_LIMIT),
        name="diff_attn",
    )(jnp.stack([lam, jnp.asarray(post_scale, F32)]).reshape(1, 2).astype(F32), qt, k, vt, g,
      subln_g.reshape(DIFF_VD, 1))


def _diff_prep_kernel(scale, q_ref, k_ref, v_ref, c_ref, s_ref, qt_ref, ko_ref, vt_ref):
    cos, sin = c_ref[...], s_ref[...]
    lane = lax.broadcasted_iota(jnp.int32, cos.shape, 1) % DIFF_HD
    half = DIFF_HD // 2

    def rope(x):
        partner = jnp.where(lane < half, pltpu.roll(x, LANES - half, 1), pltpu.roll(x, half, 1))
        return x * cos + partner * sin

    qt_ref[0] = (rope(q_ref[0]) * scale).T.astype(qt_ref.dtype)
    ko_ref[0] = rope(k_ref[0]).astype(ko_ref.dtype)
    vt_ref[0] = v_ref[0].T.astype(vt_ref.dtype)


def _diff_prep(ub, cos_t, sin_t, scale):
    b, n_all, _ = ub.shape
    tr = _pick(n_all, (768, 512, 384, 256, 128))
    col = lambda off: pl.BlockSpec((1, tr, LANES), lambda bi, i, h: (bi, i, off + h))
    tab = pl.BlockSpec((tr, LANES), lambda bi, i, h: (i, 0))
    feat = pl.BlockSpec((1, LANES, tr), lambda bi, i, h: (bi, h, i))
    return pl.pallas_call(
        functools.partial(_diff_prep_kernel, scale),
        grid=(b, n_all // tr, DIFF_HEADS),
        in_specs=[col(0), col(DIFF_HEADS), col(2 * DIFF_HEADS), tab, tab],
        out_specs=[feat, col(0), feat],
        out_shape=[jax.ShapeDtypeStruct((b, DIFF_W, n_all), BF16), jax.ShapeDtypeStruct((b, n_all, DIFF_W), BF16),
                   jax.ShapeDtypeStruct((b, DIFF_W, n_all), BF16)],
        compiler_params=pltpu.CompilerParams(dimension_semantics=("arbitrary",) * 3, vmem_limit_bytes=VMEM_LIMIT),
        name="diff_prep",
    )(ub, ub, ub, cos_t, sin_t)


SSD_GROUPS_PER_STEP = 8


def _ssd_scan_kernel(x_ref, b_ref, c_ref, dt_ref, bias_ref, alog_ref, y_ref, st_ref):
    t = SSD_CHUNK
    d = pl.program_id(1)
    gstep = pl.program_id(2)
    ngrp = x_ref.shape[2] // SSD_GW

    @pl.when(pl.program_id(3) == 0)
    def _():
        st_ref[...] = jnp.zeros_like(st_ref)

    dt = _softplus(dt_ref[0] + bias_ref[...])
    dta = dt * (-jnp.exp(alog_ref[...]))
    tr = lax.broadcasted_iota(jnp.int32, (t, t), 0)
    tc = lax.broadcasted_iota(jnp.int32, (t, t), 1)
    fwd = d == 0
    incl = jnp.where(fwd, tc - tr, tr - tc) <= 0
    x_all, b_all, c_all, st_all = x_ref[0], b_ref[0], c_ref[0], st_ref[...]
    outs = {}
    chains = []
    for j in range(ngrp):
        xs = slice(j * SSD_GW, (j + 1) * SSD_GW)
        ns = slice(j * SSD_STATE, (j + 1) * SSD_STATE)
        first_head = d * SSD_HEADS + (gstep * ngrp + j) * (SSD_HEADS // SSD_GROUPS)
        chains.append(_ssd_chain(x_all[:, xs], b_all[:, ns], c_all[:, ns], st_all[:, xs], dt, dta, first_head, fwd,
                                 incl, outs, j))
    while chains:
        alive = []
        for ch in chains:
            if next(ch, _DONE) is not _DONE:
                alive.append(ch)
        chains = alive
    for j in range(ngrp):
        xs = slice(j * SSD_GW, (j + 1) * SSD_GW)
        y, st_new = outs[j]
        y_ref[0, 0, :, xs] = y
        st_ref[:, xs] = st_new


def _ssd_chain(x, bm, cm, st, dt, dta, first_head, fwd, incl, outs, key):
    t = SSD_CHUNK
    e_heads = SSD_HEADS // SSD_GROUPS
    lane = lax.broadcasted_iota(jnp.int32, (t, 2 * SSD_HEADS), 1)
    lane_x = lax.broadcasted_iota(jnp.int32, (t, SSD_GW), 1) // SSD_HD
    lane_e = lax.broadcasted_iota(jnp.int32, (t, LANES), 1)
    dt_x = jnp.zeros((t, SSD_GW), F32)
    dta4 = jnp.zeros((t, LANES), F32)
    for e in range(e_heads):
        pick = lane == first_head + e
        dt_e = jnp.sum(jnp.where(pick, dt, 0.0), axis=1, keepdims=True)
        dta_e = jnp.sum(jnp.where(pick, dta, 0.0), axis=1, keepdims=True)
        dt_x = jnp.where(lane_x == e, dt_e, dt_x)
        dta4 = jnp.where(lane_e == e, dta_e, dta4)

    cum4 = _mm_sel(incl.astype(BF16), dta4)
    cb = _mm_nt(cm, bm)
    y_off_raw = _mm(cm, st)
    yield
    cum_t = cum4.T
    last4 = jnp.where(fwd, cum4[t - 1:t, :], cum4[0:1, :])
    cum_x = jnp.zeros((t, SSD_GW), F32)
    last_x = jnp.zeros((1, SSD_GW), F32)
    for e in range(e_heads):
        cum_x = jnp.where(lane_x == e, cum4[:, e:e + 1], cum_x)
        last_x = jnp.where(lane_x[:1] == e, last4[:, e:e + 1], last_x)

    xdt = x * dt_x
    zero_h = jnp.zeros((t, LANES), F32)
    lane_h = lax.broadcasted_iota(jnp.int32, (t, LANES), 1) < SSD_HD
    ydiag = []
    for pair in range(e_heads // 2):
        lmats = []
        for e in (2 * pair, 2 * pair + 1):
            seg = cum4[:, e:e + 1] - cum_t[e:e + 1, :]
            lmats.append(cb * jnp.exp(jnp.where(incl, seg, NEG_BIG)))
        xp = xdt[:, pair * LANES:(pair + 1) * LANES]
        xbd = jnp.concatenate([jnp.where(lane_h, xp, zero_h), jnp.where(lane_h, zero_h, xp)], axis=0)
        ydiag.append(_mm(jnp.concatenate(lmats, axis=1), xbd))
    states = _mm_tn(bm, xdt * jnp.exp(last_x - cum_x))
    yield
    y = jnp.concatenate(ydiag, axis=1) + y_off_raw * jnp.exp(cum_x)
    outs[key] = (y, st * jnp.exp(last_x) + states)


def _ssd_scan(xbc, dt_raw, dt_bias, a_log, n_ctx):
    b, n_all, _ = xbc.shape
    t = SSD_CHUNK
    nc = n_all // t
    ncc = n_ctx // t
    cidx = lambda d, i: _chunk_order(d, i, ncc, nc)
    gg = SSD_GROUPS_PER_STEP
    nsteps_g = SSD_GROUPS // gg
    xw, nw = gg * SSD_GW, gg * SSD_STATE
    b0 = SSD_W // nw
    row = pl.BlockSpec((1, 2 * SSD_HEADS), lambda bi, d, g, i: (0, 0))
    return pl.pallas_call(
        _ssd_scan_kernel,
        grid=(b, 2, nsteps_g, nc),
        in_specs=[pl.BlockSpec((1, t, xw), lambda bi, d, g, i: (bi, cidx(d, i), g)),
                  pl.BlockSpec((1, t, nw), lambda bi, d, g, i: (bi, cidx(d, i), b0 + g)),
                  pl.BlockSpec((1, t, nw), lambda bi, d, g, i: (bi, cidx(d, i), b0 + nsteps_g + g)),
                  pl.BlockSpec((1, t, 2 * SSD_HEADS), lambda bi, d, g, i: (bi, cidx(d, i), 0)),
                  row, row],
        out_specs=pl.BlockSpec((1, 1, t, xw), lambda bi, d, g, i: (bi, d, cidx(d, i), g)),
        out_shape=jax.ShapeDtypeStruct((b, 2, n_all, SSD_W), F32),
        scratch_shapes=[pltpu.VMEM((SSD_STATE, xw), F32)],
        compiler_params=pltpu.CompilerParams(dimension_semantics=("arbitrary",) * 4, vmem_limit_bytes=VMEM_LIMIT),
        name="ssd_scan",
    )(xbc, xbc, xbc, dt_raw, dt_bias.reshape(1, -1), a_log.reshape(1, -1))


def _rms_norm(x, g, eps=EPS):
    return x * lax.rsqrt(jnp.mean(x * x, axis=-1, keepdims=True) + eps) * g


HALO = 8


def _dwconv_kernel(n_ctx, n_all, silu, cur_ref, prev_ref, next_ref, w_ref, b_ref, o_ref):
    tr = cur_ref.shape[1]
    taps = w_ref.shape[0]
    pad = taps // 2
    ext = jnp.concatenate([prev_ref[0], cur_ref[0], next_ref[0]], axis=0)
    t = pl.program_id(1) * tr + lax.broadcasted_iota(jnp.int32, (tr, 1), 0)
    acc = cur_ref[0] * w_ref[pad:pad + 1, :]
    for o in range(-pad, pad + 1):
        if o == 0:
            continue
        ts = t + o
        lo, hi = (t, ts) if o > 0 else (ts, t)
        crosses = (lo < n_ctx) & (hi >= n_ctx)
        valid = (ts >= 0) & (ts < n_all) & jnp.logical_not(crosses)
        src = ext[HALO + o:HALO + o + tr]
        acc = acc + jnp.where(valid, src, 0.0) * w_ref[o + pad:o + pad + 1, :]
    if silu:
        acc = acc + b_ref[...]
        acc = acc * _sigmoid(acc)
    o_ref[0] = acc


def _dwconv(u, col0, ncols, w, bias, n_ctx, silu):
    b, n_all, _ = u.shape
    tr = _pick(n_all, (1408, 768, 512, 384, 256, 128))
    wt = _pick(math.gcd(ncols, col0) if col0 else ncols, (512, 256, 128))
    c0 = col0 // wt
    hb = tr // HALO
    last = n_all // HALO - 1
    if bias is None:
        bias = jnp.zeros((ncols,), F32)
    return pl.pallas_call(
        functools.partial(_dwconv_kernel, n_ctx, n_all, silu),
        grid=(b, n_all // tr, ncols // wt),
        in_specs=[pl.BlockSpec((1, tr, wt), lambda bi, i, j: (bi, i, c0 + j)),
                  pl.BlockSpec((1, HALO, wt), lambda bi, i, j: (bi, jnp.maximum(i * hb - 1, 0), c0 + j)),
                  pl.BlockSpec((1, HALO, wt), lambda bi, i, j: (bi, jnp.minimum((i + 1) * hb, last), c0 + j)),
                  pl.BlockSpec((w.shape[0], wt), lambda bi, i, j: (0, j)),
                  pl.BlockSpec((1, wt), lambda bi, i, j: (0, j))],
        out_specs=pl.BlockSpec((1, tr, wt), lambda bi, i, j: (bi, i, j)),
        out_shape=jax.ShapeDtypeStruct((b, n_all, ncols), F32),
        compiler_params=pltpu.CompilerParams(dimension_semantics=("arbitrary",) * 3, vmem_limit_bytes=VMEM_LIMIT),
        name="dwconv",
    )(u, u, u, w, bias.reshape(1, ncols))


def _row_select(tbl_ref, i, tm, rows_per_batch, n_ctx):
    tiles_per_batch = rows_per_batch // tm
    bi = i // tiles_per_batch
    row = (i % tiles_per_batch) * tm + lax.broadcasted_iota(jnp.int32, (tm, 1), 0)
    nb = tbl_ref.shape[0] - 1
    return jnp.where(row < n_ctx, tbl_ref[nb:nb + 1, :], tbl_ref[pl.ds(bi, 1), :])


def _prenorm_kernel(n_ctx, rows_per_batch, x_ref, g_ref, shift_ref, scale_ref, o_ref):
    i = pl.program_id(0)
    tm = x_ref.shape[0]
    x = x_ref[...]
    y = x * lax.rsqrt(jnp.mean(x * x, axis=-1, keepdims=True) + EPS) * g_ref[...]
    scale = _row_select(scale_ref, i, tm, rows_per_batch, n_ctx)
    shift = _row_select(shift_ref, i, tm, rows_per_batch, n_ctx)
    o_ref[...] = (y * (1.0 + scale) + shift).astype(o_ref.dtype)


def _prenorm(x2d, g, shift_tbl, scale_tbl, n_ctx, rows_per_batch):
    m, n = x2d.shape
    tm = _pick(rows_per_batch, (384, 256, 128))
    tbl = pl.BlockSpec(shift_tbl.shape, lambda i: (0, 0))
    return pl.pallas_call(
        functools.partial(_prenorm_kernel, n_ctx, rows_per_batch),
        grid=(m // tm,),
        in_specs=[pl.BlockSpec((tm, n), lambda i: (i, 0)), pl.BlockSpec((1, n), lambda i: (0, 0)), tbl, tbl],
        out_specs=pl.BlockSpec((tm, n), lambda i: (i, 0)),
        out_shape=jax.ShapeDtypeStruct((m, n), BF16),
        compiler_params=pltpu.CompilerParams(dimension_semantics=("arbitrary",), vmem_limit_bytes=VMEM_LIMIT),
        name="prenorm",
    )(x2d, g.reshape(1, n), shift_tbl, scale_tbl)


def _rwkv_finish_kernel(yf_ref, yb_ref, bonus_ref, g_ref, w_ref, b_ref, o_ref):
    y = yf_ref[0] + yb_ref[0]
    hr = lax.broadcasted_iota(jnp.int32, (LANES, LANES), 0) // RWKV_HD
    hc = lax.broadcasted_iota(jnp.int32, (LANES, LANES), 1) // RWKV_HD
    head_ones = (hr == hc).astype(BF16)
    outs = []
    for p in range(y.shape[1] // LANES):
        yp = y[:, p * LANES:(p + 1) * LANES]
        mu = _mm_sel_right(yp, head_ones) * (1.0 / RWKV_HD)
        dev = yp - mu
        var = _mm_sel_right(dev * dev, head_ones) * (1.0 / RWKV_HD)
        outs.append(dev * lax.rsqrt(var + RWKV_GN_EPS))
    yn = jnp.concatenate(outs, axis=1) * w_ref[...] + b_ref[...]
    g = g_ref[0]
    o_ref[0] = ((yn + bonus_ref[0]) * (g * _sigmoid(g))).astype(o_ref.dtype)


def _rwkv_finish(yf, yb, bonus, ua, g_col0, lnx_w, lnx_b):
    b, n_all, _ = yf.shape
    tr = _pick(n_all, (768, 512, 384, 256, 128))
    wt = 256
    g0 = g_col0 // wt
    vec = pl.BlockSpec((1, wt), lambda bi, i, j: (0, j))
    blk = lambda c0: pl.BlockSpec((1, tr, wt), lambda bi, i, j: (bi, i, c0 + j))
    return pl.pallas_call(
        _rwkv_finish_kernel,
        grid=(b, n_all // tr, RWKV_W // wt),
        in_specs=[blk(0), blk(0), blk(0), blk(g0), vec, vec],
        out_specs=blk(0),
        out_shape=jax.ShapeDtypeStruct((b, n_all, RWKV_W), BF16),
        compiler_params=pltpu.CompilerParams(dimension_semantics=("arbitrary",) * 3, vmem_limit_bytes=VMEM_LIMIT),
        name="rwkv_finish",
    )(yf, yb, bonus, ua, lnx_w.reshape(1, -1), lnx_b.reshape(1, -1))


def _ssd_finish_kernel(y_ref, x_ref, z_ref, d_ref, g_ref, o_ref):
    z = z_ref[0]
    y = (y_ref[0, 0] + y_ref[0, 1] + d_ref[...] * x_ref[0]) * (z * _sigmoid(z))
    outs = []
    for grp in range(y.shape[1] // SSD_GW):
        yg = y[:, grp * SSD_GW:(grp + 1) * SSD_GW]
        outs.append(yg * lax.rsqrt(jnp.mean(yg * yg, axis=-1, keepdims=True) + EPS))
    o_ref[0] = (jnp.concatenate(outs, axis=1) * g_ref[...]).astype(o_ref.dtype)


def _ssd_finish(y2, xact, uc, d_vec, norm_g):
    b, _, n_all, _ = y2.shape
    tr = _pick(n_all, (384, 256, 128))
    wt = 1024
    vec = pl.BlockSpec((1, wt), lambda bi, i, j: (0, j))
    blk = pl.BlockSpec((1, tr, wt), lambda bi, i, j: (bi, i, j))
    return pl.pallas_call(
        _ssd_finish_kernel,
        grid=(b, n_all // tr, SSD_W // wt),
        in_specs=[pl.BlockSpec((1, 2, tr, wt), lambda bi, i, j: (bi, 0, i, j)), blk, blk, vec, vec],
        out_specs=blk,
        out_shape=jax.ShapeDtypeStruct((b, n_all, SSD_W), BF16),
        compiler_params=pltpu.CompilerParams(dimension_semantics=("arbitrary",) * 3, vmem_limit_bytes=VMEM_LIMIT),
        name="ssd_finish",
    )(y2, xact, uc, d_vec.reshape(1, -1), norm_g.reshape(1, -1))


def _axial_rope(n_tok):
    rows = n_tok // GRID_W
    row = jnp.repeat(jnp.arange(rows, dtype=F32), GRID_W)
    col = jnp.tile(jnp.arange(GRID_W, dtype=F32), rows)
    n_freq = DIFF_HD // 4
    inv = ROPE_BASE ** (-jnp.arange(n_freq, dtype=F32) / n_freq)
    ang = jnp.concatenate([row[:, None] * inv, col[:, None] * inv], axis=-1)
    return jnp.cos(ang), jnp.sin(ang)


def _rwkv_branch(ua, n_ctx, conv_w, w0, w2, a0, a2, k_k, k_a, r_k, lnx_w, lnx_b):
    uconv = _dwconv(ua, 0, RWKV_CONV_CH, conv_w, None, n_ctx, False)
    m, n, q, yl, bonus = _rwkv_local(uconv, w0, w2, a0, a2, k_k, k_a, r_k)
    yf, yb = _rwkv_scan(m, n, q, yl, n_ctx)
    return _rwkv_finish(yf, yb, bonus, ua, RWKV_CONV_CH, lnx_w, lnx_b)


def _diff_branch(ub, n_ctx, cos_t, sin_t, lam_p, subln_g, lam_init):
    lam = jnp.exp(jnp.sum(lam_p[0] * lam_p[1])) - jnp.exp(jnp.sum(lam_p[2] * lam_p[3])) + lam_init
    scale = DIFF_HD ** -0.5 * math.log2(math.e)
    qt, k_all, vt = _diff_prep(ub, cos_t, sin_t, scale)
    g = ub[..., 3 * DIFF_W:]
    post = 1.0 - lam_init
    oc = _diff_attn(qt[:, :, :n_ctx], k_all[:, :n_ctx], vt[:, :, :n_ctx], g[:, :n_ctx], subln_g, lam, post)
    ol = _diff_attn(qt[:, :, n_ctx:], k_all, vt, g[:, n_ctx:], subln_g, lam, post)
    return jnp.concatenate([oc, ol], axis=1)


def _rope_tables(n_ctx, n_lat):
    cos, sin = _axial_rope(n_lat)
    cos4 = jnp.tile(cos, (1, 4))
    sin4 = jnp.tile(jnp.concatenate([-sin, sin], axis=1), (1, 2))
    ident = lambda v: jnp.full((n_ctx, LANES), v, F32)
    return jnp.concatenate([ident(1.0), cos4], axis=0), jnp.concatenate([ident(0.0), sin4], axis=0)


def _ssd_branch(uc, dt_raw, n_ctx, conv_w, conv_b, a_log, dt_bias, d_skip, norm_g):
    xact = _dwconv(uc, SSD_W, SSD_XBC, conv_w, conv_b, n_ctx, True)
    y2 = _ssd_scan(xact, dt_raw, dt_bias, a_log, n_ctx)
    return _ssd_finish(y2, xact, uc, jnp.repeat(d_skip, SSD_HD), norm_g)


def kernel(x, c, ctx, c_ctx, w_mod, b_mod, g_pre, g_post, w_in, rwkv_conv, rwkv_w0, rwkv_w2, rwkv_a0, rwkv_a2,
           rwkv_k_k, rwkv_k_a, rwkv_r_k, rwkv_lnx_w, rwkv_lnx_b, diff_lambda, diff_subln, ssd_conv_w, ssd_conv_b,
           ssd_a_log, ssd_dt_bias, ssd_d, ssd_norm, w_branch_a, w_branch_b, w_branch_c, w_out):
    b, n_lat, dm = x.shape
    n_ctx = ctx.shape[1]
    n_all = n_ctx + n_lat
    depth = w_in.shape[0]
    cos_t, sin_t = _rope_tables(n_ctx, n_lat)
    cond = jax.nn.silu(jnp.concatenate([c, c_ctx[None, :]], axis=0))
    cond = jnp.pad(cond, ((0, 8 - (b + 1)), (0, 0)))

    o_rw, o_g, o_dt = 0, RWKV_CONV_CH, RWKV_CONV_CH + RWKV_W
    src_dt = o_dt + 4 * DIFF_W + SSD_W + SSD_XBC
    pad_a = -(o_dt + 2 * SSD_HEADS) % 512

    xa = jnp.concatenate([ctx, x], axis=1)
    for li in range(depth):
        lam_init = 0.8 - 0.6 * math.exp(-0.3 * li)
        mod = _matmul(cond, w_mod, tn_cands=(1024, 512, 256, 128), layer=li)[:b + 1] + b_mod[li]
        shift_l, scale_l, gate_l = jnp.split(mod[:b], 3, axis=-1)
        shift_c, scale_c, gate_c = jnp.split(mod[b], 3, axis=-1)
        tbl = lambda vl, vc: jnp.concatenate([vl, vc[None, :]], axis=0)
        hb = _prenorm(xa.reshape(b * n_all, dm), g_pre[li], tbl(shift_l, shift_c), tbl(scale_l, scale_c), n_ctx,
                      n_all)
        wl = w_in[li]
        proj = lambda w: _matmul(hb, w.astype(BF16), tn_cands=(1024, 512, 256, 128)).reshape(b, n_all, -1)
        ua = proj(jnp.concatenate([wl[:, :o_dt], wl[:, src_dt:src_dt + 2 * SSD_HEADS],
                                   jnp.zeros((dm, pad_a), wl.dtype)], axis=1))
        ub = proj(wl[:, o_dt:o_dt + 4 * DIFF_W])
        uc = proj(wl[:, o_dt + 4 * DIFF_W:src_dt])
        ug = proj(wl[:, src_dt + 2 * SSD_HEADS:])

        oa = _rwkv_branch(ua, n_ctx, rwkv_conv[li], rwkv_w0[li], rwkv_w2[li], rwkv_a0[li], rwkv_a2[li],
                          rwkv_k_k[li], rwkv_k_a[li], rwkv_r_k[li], rwkv_lnx_w[li], rwkv_lnx_b[li])
        ob = _diff_branch(ub, n_ctx, cos_t, sin_t, diff_lambda[li], diff_subln[li], lam_init)
        oc = _ssd_branch(uc, ua[..., o_dt:o_dt + 2 * SSD_HEADS], n_ctx, ssd_conv_w[li], ssd_conv_b[li],
                         ssd_a_log[li], ssd_dt_bias[li], ssd_d[li], ssd_norm[li])

        flat = lambda t: t.astype(BF16).reshape(b * n_all, -1)
        mrg = _merge(flat(oa), flat(ob), flat(oc), w_branch_a[li].astype(BF16), w_branch_b[li].astype(BF16),
                     w_branch_c[li].astype(BF16), ug.reshape(b * n_all, 3 * dm))
        xa = _out_norm(mrg, w_out[li].astype(BF16), xa.reshape(b * n_all, dm), tbl(gate_l, gate_c), g_post[li],
                       n_ctx, n_all).reshape(b, n_all, dm)
    return xa[:, n_ctx:]
```

```python
import functools
import math

import jax
import jax.numpy as jnp
from jax import lax
from jax.experimental import pallas as pl
from jax.experimental.pallas import tpu as pltpu

F32 = jnp.float32
BF16 = jnp.bfloat16

EPS = 1e-6
GRID_W = 64
ROPE_BASE = 10000.0

RWKV_HEADS = 16
RWKV_HD = 64
RWKV_W = RWKV_HEADS * RWKV_HD
RWKV_LORA = 64
RWKV_CONV_CH = 3 * RWKV_W + 4 * RWKV_LORA
RWKV_GN_EPS = 64e-5
RWKV_CHUNK = 64
RWKV_LOCAL_WIDTH = 1024

DIFF_HEADS = 8
DIFF_HD = 64
DIFF_VD = 2 * DIFF_HD
DIFF_W = DIFF_HEADS * DIFF_VD

SSD_HEADS = 32
SSD_HD = 64
SSD_W = SSD_HEADS * SSD_HD
SSD_GROUPS = 8
SSD_STATE = 128
SSD_CHUNK = 128
SSD_XBC = SSD_W + 2 * SSD_GROUPS * SSD_STATE
SSD_GW = SSD_W // SSD_GROUPS

LANES = 128
VMEM_LIMIT = 56 * 1024 * 1024

NEG_BIG = -1e30


def _pick(n, candidates):
    for c in candidates:
        if n % c == 0:
            return c
    return n


def _mm(a, b):
    return jnp.dot(a.astype(BF16), b.astype(BF16), preferred_element_type=F32)


def _mm_nt(a, b):
    return lax.dot_general(a.astype(BF16), b.astype(BF16), (((1,), (1,)), ((), ())), preferred_element_type=F32)


def _mm_tn(a, b):
    return lax.dot_general(a.astype(BF16), b.astype(BF16), (((0,), (0,)), ((), ())), preferred_element_type=F32)


def _split3(x):
    x1 = x.astype(BF16)
    r1 = x - x1.astype(F32)
    x2 = r1.astype(BF16)
    x3 = (r1 - x2.astype(F32)).astype(BF16)
    return x1, x2, x3


def _mm_sel(sel, x):
    x1, x2, x3 = _split3(x)
    d = lambda y: jnp.dot(sel, y, preferred_element_type=F32)
    return d(x1) + d(x2) + d(x3)


def _softplus(x):
    return jnp.maximum(x, 0.0) + jnp.log(1.0 + jnp.exp(-jnp.abs(x)))


def _sigmoid(x):
    return 1.0 / (1.0 + jnp.exp(-x))


def _matmul_kernel(a_ref, w_ref, o_ref):
    o_ref[...] = _mm(a_ref[...], w_ref[...]).astype(o_ref.dtype)


def _matmul(a, w, out_dtype=F32, tm_cands=(768, 512, 384, 256, 128, 8), tn_cands=(512, 256, 128), layer=None):
    m, k = a.shape
    n = w.shape[-1]
    tm = _pick(m, tm_cands)
    tn = _pick(n, tn_cands)
    if layer is None:
        w_spec = pl.BlockSpec((k, tn), lambda i, j: (0, j))
    else:
        w_spec = pl.BlockSpec((None, k, tn), lambda i, j: (layer, 0, j))
    return pl.pallas_call(
        _matmul_kernel,
        grid=(m // tm, n // tn),
        in_specs=[pl.BlockSpec((tm, k), lambda i, j: (i, 0)), w_spec],
        out_specs=pl.BlockSpec((tm, tn), lambda i, j: (i, j)),
        out_shape=jax.ShapeDtypeStruct((m, n), out_dtype),
        compiler_params=pltpu.CompilerParams(dimension_semantics=("arbitrary", "arbitrary"),
                                             vmem_limit_bytes=VMEM_LIMIT),
        name="dense_matmul",
    )(a, w)


def _merge_kernel(oa_ref, ob_ref, oc_ref, wa_ref, wb_ref, wc_ref, sa_ref, sb_ref, sc_ref, o_ref):
    dot = lambda x_ref, w_ref: jnp.dot(x_ref[...], w_ref[...], preferred_element_type=F32)
    o = (_sigmoid(sa_ref[...]) * dot(oa_ref, wa_ref) + _sigmoid(sb_ref[...]) * dot(ob_ref, wb_ref)
         + _sigmoid(sc_ref[...]) * dot(oc_ref, wc_ref))
    o_ref[...] = o.astype(o_ref.dtype)


def _merge(oa, ob, oc, wa, wb, wc, gates):
    m = oa.shape[0]
    n = wa.shape[1]
    tm = _pick(m, (768, 512, 384, 256, 128))
    tn = _pick(n, (512, 256, 128))
    nj = n // tn
    act = lambda t: pl.BlockSpec((tm, t.shape[1]), lambda i, j: (i, 0))
    wgt = lambda t: pl.BlockSpec((t.shape[0], tn), lambda i, j: (0, j))
    gate = lambda s: pl.BlockSpec((tm, tn), lambda i, j: (i, s * nj + j))
    return pl.pallas_call(
        _merge_kernel,
        grid=(m // tm, nj),
        in_specs=[act(oa), act(ob), act(oc), wgt(wa), wgt(wb), wgt(wc), gate(0), gate(1), gate(2)],
        out_specs=pl.BlockSpec((tm, tn), lambda i, j: (i, j)),
        out_shape=jax.ShapeDtypeStruct((m, n), BF16),
        compiler_params=pltpu.CompilerParams(dimension_semantics=("arbitrary", "arbitrary"),
                                             vmem_limit_bytes=VMEM_LIMIT),
        name="branch_merge",
    )(oa, ob, oc, wa, wb, wc, gates, gates, gates)


def _out_norm_kernel(n_ctx, rows_per_batch, m_ref, w_ref, x_ref, gate_ref, g_ref, o_ref, acc_ref):
    i = pl.program_id(0)
    kk = pl.program_id(1)

    @pl.when(kk == 0)
    def _():
        acc_ref[...] = jnp.zeros_like(acc_ref)

    acc_ref[...] += jnp.dot(m_ref[...], w_ref[...], preferred_element_type=F32)

    @pl.when(kk == pl.num_programs(1) - 1)
    def _():
        tm = acc_ref.shape[0]
        out = acc_ref[...]
        y = out * lax.rsqrt(jnp.mean(out * out, axis=-1, keepdims=True) + EPS) * g_ref[...]
        tiles_per_batch = rows_per_batch // tm
        bi = i // tiles_per_batch
        row = (i % tiles_per_batch) * tm + lax.broadcasted_iota(jnp.int32, (tm, 1), 0)
        nb = gate_ref.shape[0] - 1
        gate = jnp.where(row < n_ctx, gate_ref[nb:nb + 1, :], gate_ref[pl.ds(bi, 1), :])
        o_ref[...] = x_ref[...] + gate * y


def _out_norm(mrg, w, x2d, gates, g_post, n_ctx, rows_per_batch):
    m, k = mrg.shape
    n = w.shape[1]
    tm = _pick(rows_per_batch, (384, 256, 128))
    tk = _pick(k, (512, 256, 128))
    return pl.pallas_call(
        functools.partial(_out_norm_kernel, n_ctx, rows_per_batch),
        grid=(m // tm, k // tk),
        in_specs=[pl.BlockSpec((tm, tk), lambda i, kk: (i, kk)),
                  pl.BlockSpec((tk, n), lambda i, kk: (kk, 0)),
                  pl.BlockSpec((tm, n), lambda i, kk: (i, 0)),
                  pl.BlockSpec(gates.shape, lambda i, kk: (0, 0)),
                  pl.BlockSpec((1, n), lambda i, kk: (0, 0))],
        out_specs=pl.BlockSpec((tm, n), lambda i, kk: (i, 0)),
        out_shape=jax.ShapeDtypeStruct((m, n), F32),
        scratch_shapes=[pltpu.VMEM((tm, n), F32)],
        compiler_params=pltpu.CompilerParams(dimension_semantics=("arbitrary", "arbitrary"),
                                             vmem_limit_bytes=VMEM_LIMIT),
        name="out_norm",
    )(mrg, w, x2d, gates, g_post.reshape(1, n))


def _head_masks(rows):
    lane = lax.broadcasted_iota(jnp.int32, (rows, LANES), 1)
    return lane < RWKV_HD


def _bd(x, first):
    zero = jnp.zeros_like(x)
    return jnp.concatenate([jnp.where(first, x, zero), jnp.where(first, zero, x)], axis=0)


def _rwkv_local_kernel(r_ref, k_ref, v_ref, lw_ref, la_ref, w0_ref, w2_ref, a0_ref, a2_ref, kk_ref, ka_ref,
                       rk_ref, m_ref, n_ref, q_ref, yl_ref, bonus_ref):
    c = RWKV_CHUNK
    width = r_ref.shape[2]
    first = _head_masks(c)
    row = lax.broadcasted_iota(jnp.int32, (c, LANES), 0)
    col = lax.broadcasted_iota(jnp.int32, (c, LANES), 1) % RWKV_HD
    eye_cat = (row == col).astype(F32)
    hr = lax.broadcasted_iota(jnp.int32, (LANES, LANES), 0) // RWKV_HD
    hc = lax.broadcasted_iota(jnp.int32, (LANES, LANES), 1) // RWKV_HD
    head_ones = (hr == hc).astype(BF16)
    tr = lax.broadcasted_iota(jnp.int32, (c, c), 0)
    tc = lax.broadcasted_iota(jnp.int32, (c, c), 1)
    zrow = jnp.zeros((RWKV_LORA, width), F32)

    tanh_lw = jnp.tanh(lw_ref[0])
    la = la_ref[0]
    w_raw2, a_sig2 = [], []
    for d in (0, 1):
        pad = (lambda m: jnp.concatenate([m, zrow], axis=0)) if d == 0 else (lambda m: jnp.concatenate([zrow, m], axis=0))
        w_raw2.append(w0_ref[d:d + 1, :] + _mm(tanh_lw, pad(w2_ref[d])))
        a_sig2.append(_sigmoid(a0_ref[d:d + 1, :] + _mm(la, pad(a2_ref[d]))))

    r_all, k_all, v_all = r_ref[0], k_ref[0], v_ref[0]
    kk_all, ka_all, rk_all = kk_ref[...], ka_ref[...], rk_ref[...]
    consts = (first, row, col, eye_cat, tr, tc)
    chains, outs = [], {}
    for pi in range(width // LANES):
        sl = slice(pi * LANES, (pi + 1) * LANES)
        r, k, v = r_all[:, sl], k_all[:, sl], v_all[:, sl]
        kkp = k * kk_all[:, sl]
        kk = kkp / jnp.maximum(jnp.sqrt(_mm_sel_right(kkp * kkp, head_ones)), 1e-12)
        a_sum = a_sig2[0][:, sl] + a_sig2[1][:, sl]
        kd_sum = k * (2.0 + (a_sum - 2.0) * ka_all[:, sl])
        outs[pi] = _mm_sel_right(r * kd_sum * rk_all[:, sl], head_ones) * v
        for d in (0, 1):
            chains.append(_rwkv_chain(d, r, k, v, kk, w_raw2[d][:, sl], a_sig2[d][:, sl], ka_all[:, sl], consts,
                                      outs, (pi, d)))
    while chains:
        alive = []
        for ch in chains:
            if next(ch, _DONE) is not _DONE:
                alive.append(ch)
        chains = alive
    for pi in range(width // LANES):
        sl = slice(pi * LANES, (pi + 1) * LANES)
        for d in (0, 1):
            m_cat, n_cat, q, yloc = outs[(pi, d)]
            m_ref[0, d, 0, :, sl] = m_cat
            n_ref[0, d, 0, :, sl] = n_cat
            q_ref[0, d, :, sl] = q
            yl_ref[0, d, :, sl] = yloc
        bonus_ref[0, :, sl] = outs[pi]


_DONE = object()


def _rwkv_chain(d, r, k, v, kk, w_raw, a_sig, ka, consts, outs, key):
    c = RWKV_CHUNK
    first, row, col, eye_cat, tr, tc = consts
    if d == 0:
        incl, strict, tri = col <= row, col < row, (tc <= tr)
    else:
        incl, strict, tri = col >= row, col > row, (tc >= tr)
    logw = -jnp.exp(-_softplus(-w_raw) - 0.5)
    k_dir = k * (1.0 + (a_sig - 1.0) * ka)
    b_dir = kk * a_sig

    cum = _mm_sel(tri.astype(BF16), logw)
    yield
    tot = cum[c - 1:c, :] if d == 0 else cum[0:1, :]
    g_incl = jnp.exp(cum)
    g_excl = jnp.exp(cum - logw)
    g_inv = jnp.exp(-cum)
    g_hat = jnp.exp(tot - cum)
    at = -kk * g_excl
    rt = r * g_incl
    bt = b_dir * g_inv
    kt = k_dir * g_inv
    bh = b_dir * g_hat
    kh = k_dir * g_hat

    sc = _mm_nt(jnp.concatenate([at, rt], axis=0),
                jnp.concatenate([_bd(bt, first), _bd(kt, first)], axis=0))
    yield
    zero = jnp.zeros((c, LANES), F32)
    a_ab = jnp.where(strict, sc[:c, :LANES], zero)
    a_ak = jnp.where(strict, sc[:c, LANES:], zero)
    a_rb = jnp.where(incl, sc[c:, :LANES], zero)
    a_rk = jnp.where(incl, sc[c:, LANES:], zero)

    t = eye_cat + a_ab
    p = _mm(a_ab, _bd(a_ab, first))
    akv = _mm(a_ak, _bd(v, first))
    yield
    for _ in range(4):
        pbd = _bd(p, first)
        t, p = t + _mm(t, pbd), _mm(p, pbd)
        yield
    t = t + _mm(t, _bd(p, first))
    yield

    x = _mm(t, jnp.concatenate([_bd(akv, first), _bd(at, first)], axis=1))
    yield
    uloc = x[:, :LANES]
    w = x[:, LANES:]
    yloc = _mm(jnp.concatenate([a_rb, a_rk], axis=1),
               jnp.concatenate([_bd(uloc, first), _bd(v, first)], axis=0))
    q = rt + _mm(a_rb, _bd(w, first))
    g = _mm_tn(jnp.concatenate([bh, kh], axis=1),
               jnp.concatenate([uloc, w, v], axis=1))
    yield
    nfull = g[:LANES, :LANES] + g[LANES:, 2 * LANES:]
    mfull = g[:LANES, LANES:2 * LANES]
    n_cat = jnp.where(first, nfull[:c], zero) + jnp.where(first, zero, nfull[c:])
    m_cat = jnp.where(first, mfull[:c], zero) + jnp.where(first, zero, mfull[c:]) + eye_cat * jnp.exp(tot)
    outs[key] = (m_cat, n_cat, q, yloc)


def _mm_sel_right(x, sel):
    x1, x2, x3 = _split3(x)
    d = lambda y: jnp.dot(y, sel, preferred_element_type=F32)
    return d(x1) + d(x2) + d(x3)


def _rwkv_local(uconv, w0, w2, a0, a2, k_k, k_a, r_k):
    b, n_all, _ = uconv.shape
    c = RWKV_CHUNK
    nc = n_all // c
    wb = RWKV_LOCAL_WIDTH
    nblk = RWKV_W // wb
    row = lambda t: t.reshape(1, RWKV_W)
    col_blk = lambda off: pl.BlockSpec((1, c, wb), lambda bi, ci, pi: (bi, ci, off + pi))
    lora_blk = lambda off: pl.BlockSpec((1, c, LANES), lambda bi, ci, pi: (bi, ci, off))
    vec2 = pl.BlockSpec((2, wb), lambda bi, ci, pi: (0, pi))
    mat2 = pl.BlockSpec((2, RWKV_LORA, wb), lambda bi, ci, pi: (0, 0, pi))
    vec1 = pl.BlockSpec((1, wb), lambda bi, ci, pi: (0, pi))
    mn_shape = jax.ShapeDtypeStruct((b, 2, nc, RWKV_HD, RWKV_W), F32)
    mn_spec = pl.BlockSpec((1, 2, 1, RWKV_HD, wb), lambda bi, ci, pi: (bi, 0, ci, 0, pi))
    qy_shape = jax.ShapeDtypeStruct((b, 2, n_all, RWKV_W), F32)
    qy_spec = pl.BlockSpec((1, 2, c, wb), lambda bi, ci, pi: (bi, 0, ci, pi))
    lora0 = 3 * RWKV_W // LANES
    return pl.pallas_call(
        _rwkv_local_kernel,
        grid=(b, nc, nblk),
        in_specs=[col_blk(0), col_blk(nblk), col_blk(2 * nblk), lora_blk(lora0), lora_blk(lora0 + 1),
                  vec2, mat2, vec2, mat2, vec1, vec1, vec1],
        out_specs=[mn_spec, mn_spec, qy_spec, qy_spec,
                   pl.BlockSpec((1, c, wb), lambda bi, ci, pi: (bi, ci, pi))],
        out_shape=[mn_shape, mn_shape, qy_shape, qy_shape, jax.ShapeDtypeStruct((b, n_all, RWKV_W), F32)],
        compiler_params=pltpu.CompilerParams(dimension_semantics=("arbitrary",) * 3, vmem_limit_bytes=VMEM_LIMIT),
        name="rwkv_local",
    )(uconv, uconv, uconv, uconv, uconv, w0, w2, a0, a2, row(k_k), row(k_a), row(r_k))


def _rwkv_scan_kernel(mf_ref, nf_ref, qf_ref, ylf_ref, mb_ref, nb_ref, qb_ref, ylb_ref, yf_ref, yb_ref, s_ref):
    c = RWKV_CHUNK

    @pl.when(pl.program_id(1) == 0)
    def _():
        s_ref[...] = jnp.zeros_like(s_ref)

    first = _head_masks(RWKV_HD)
    dot = lambda x, y: jnp.dot(x, y, preferred_element_type=F32)
    for d, (m_ref, n_ref, q_ref, yl_ref, y_ref) in enumerate(((mf_ref, nf_ref, qf_ref, ylf_ref, yf_ref),
                                                             (mb_ref, nb_ref, qb_ref, ylb_ref, yb_ref))):
        for p in range(RWKV_W // LANES):
            sl = slice(p * LANES, (p + 1) * LANES)
            s1, s2, s3 = _split3(_bd(s_ref[d, :, sl], first))
            lhs = jnp.concatenate([q_ref[0, 0, :, sl], m_ref[0, 0, 0, :, sl]], axis=0)
            l1, l2, l3 = _split3(lhs)
            z = dot(l1, s1) + (dot(l1, s2) + dot(l2, s1)) + (dot(l2, s2) + dot(l1, s3) + dot(l3, s1))
            y_ref[0, :, sl] = yl_ref[0, 0, :, sl] + z[:c]
            s_ref[d, :, sl] = z[c:] + n_ref[0, 0, 0, :, sl]


def _chunk_order(d, i, n_ctx_chunks, n_chunks):
    back = jnp.where(i < n_ctx_chunks, n_ctx_chunks - 1 - i, n_chunks + n_ctx_chunks - 1 - i)
    return jnp.where(d == 0, i, back)


def _rwkv_scan(m, n, q, yl, n_ctx):
    b, _, nc, _, _ = m.shape
    n_all = q.shape[2]
    c = RWKV_CHUNK
    ncc = n_ctx // c
    cidx = lambda d, i: _chunk_order(d, i, ncc, nc)
    mn_spec = lambda d: pl.BlockSpec((1, 1, 1, RWKV_HD, RWKV_W), lambda bi, i: (bi, d, cidx(d, i), 0, 0))
    qy_spec = lambda d: pl.BlockSpec((1, 1, c, RWKV_W), lambda bi, i: (bi, d, cidx(d, i), 0))
    y_spec = lambda d: pl.BlockSpec((1, c, RWKV_W), lambda bi, i: (bi, cidx(d, i), 0))
    y_shape = jax.ShapeDtypeStruct((b, n_all, RWKV_W), F32)
    return pl.pallas_call(
        _rwkv_scan_kernel,
        grid=(b, nc),
        in_specs=[mn_spec(0), mn_spec(0), qy_spec(0), qy_spec(0), mn_spec(1), mn_spec(1), qy_spec(1), qy_spec(1)],
        out_specs=[y_spec(0), y_spec(1)],
        out_shape=[y_shape, y_shape],
        scratch_shapes=[pltpu.VMEM((2, RWKV_HD, RWKV_W), F32)],
        compiler_params=pltpu.CompilerParams(dimension_semantics=("arbitrary",) * 2, vmem_limit_bytes=VMEM_LIMIT),
        name="rwkv_scan",
    )(m, n, q, yl, m, n, q, yl)


ATTN_COLS = 256


def _diff_attn_kernel(tsub, lam_ref, qt_ref, k_ref, vt_ref, g_ref, subln_ref, o_ref, qs_ref, m_ref, l_ref, acc_ref):
    tq = qt_ref.shape[2]
    tk = k_ref.shape[1]
    kv = pl.program_id(3)

    @pl.when(kv == 0)
    def _():
        qt = qt_ref[0]
        feat = lax.broadcasted_iota(jnp.int32, qt.shape, 0)
        zero = jnp.zeros_like(qt)
        qs_ref[...] = jnp.concatenate([jnp.where(feat < DIFF_HD, qt, zero), jnp.where(feat < DIFF_HD, zero, qt)], axis=1)
        m_ref[...] = jnp.full_like(m_ref, NEG_BIG)
        l_ref[...] = jnp.zeros_like(l_ref)
        acc_ref[...] = jnp.zeros_like(acc_ref)

    nsub = tk // tsub
    strips = [slice(cb * ATTN_COLS, (cb + 1) * ATTN_COLS) for cb in range(2 * tq // ATTN_COLS)]
    ones = jnp.ones((ATTN_ONES_ROWS, tsub), BF16)

    def scores_of(j):
        kj = k_ref[0, j * tsub:(j + 1) * tsub, :]
        return [jnp.dot(kj, qs_ref[:, sl], preferred_element_type=F32) for sl in strips]

    nxt = scores_of(0)
    for j in range(nsub):
        scores = nxt
        if j + 1 < nsub:
            nxt = scores_of(j + 1)
        v_ext = jnp.concatenate([vt_ref[0, :, j * tsub:(j + 1) * tsub], ones], axis=0)
        for sl, s in zip(strips, scores):
            m_old = m_ref[:, sl]
            m_new = jnp.maximum(m_old, jnp.max(s, axis=0, keepdims=True))
            alpha = jnp.exp2(m_old - m_new)
            p = jnp.exp2(s - m_new[:1]).astype(BF16)
            pv = jnp.dot(v_ext, p, preferred_element_type=F32)
            acc_ref[:, sl] = alpha[:1] * acc_ref[:, sl] + pv[:DIFF_VD]
            l_ref[:, sl] = alpha * l_ref[:, sl] + pv[DIFF_VD:DIFF_VD + 8]
            m_ref[:, sl] = m_new

    @pl.when(kv == pl.num_programs(3) - 1)
    def _():
        o = acc_ref[...] / l_ref[:1, :]
        diff = o[:, :tq] - lam_ref[0, 0] * o[:, tq:]
        inv = lax.rsqrt(jnp.mean(diff * diff, axis=0, keepdims=True) + EPS)
        y = (diff * inv * (subln_ref[...] * lam_ref[0, 1])).T
        g = g_ref[0]
        o_ref[0] = (y * (g * _sigmoid(g))).astype(o_ref.dtype)


ATTN_ONES_ROWS = 16
ATTN_MAX_SUBTILES = 11


def _diff_attn(qt, k, vt, g, subln_g, lam, post_scale):
    b, _, nq = qt.shape
    nk = k.shape[1]
    tq = _pick(nq, (1024, 512, 256, 128))
    tsub = _pick(nk, (768, 512, 384, 256, 128))
    n_sub = nk // tsub
    tk = tsub * max(s for s in range(1, ATTN_MAX_SUBTILES + 1) if n_sub % s == 0)
    return pl.pallas_call(
        functools.partial(_diff_attn_kernel, tsub),
        grid=(b, DIFF_HEADS, nq // tq, nk // tk),
        in_specs=[pl.BlockSpec(memory_space=pltpu.SMEM),
                  pl.BlockSpec((1, LANES, tq), lambda bi, h, i, j: (bi, h, i)),
                  pl.BlockSpec((1, tk, LANES), lambda bi, h, i, j: (bi, j, h)),
                  pl.BlockSpec((1, LANES, tk), lambda bi, h, i, j: (bi, h, j)),
                  pl.BlockSpec((1, tq, LANES), lambda bi, h, i, j: (bi, i, h)),
                  pl.BlockSpec((DIFF_VD, 1), lambda bi, h, i, j: (0, 0))],
        out_specs=pl.BlockSpec((1, tq, LANES), lambda bi, h, i, j: (bi, i, h)),
        out_shape=jax.ShapeDtypeStruct((b, nq, DIFF_W), BF16),
        scratch_shapes=[pltpu.VMEM((LANES, 2 * tq), BF16), pltpu.VMEM((8, 2 * tq), F32),
                        pltpu.VMEM((8, 2 * tq), F32), pltpu.VMEM((DIFF_VD, 2 * tq), F32)],
        compiler_params=pltpu.CompilerParams(dimension_semantics=("arbitrary",) * 4, vmem_limit_bytes=VMEM_LIMIT),
        name="diff_attn",
    )(jnp.stack([lam, jnp.asarray(post_scale, F32)]).reshape(1, 2).astype(F32), qt, k, vt, g,
      subln_g.reshape(DIFF_VD, 1))


def _diff_prep_kernel(scale, q_ref, k_ref, v_ref, c_ref, s_ref, qt_ref, ko_ref, vt_ref):
    cos, sin = c_ref[...], s_ref[...]
    lane = lax.broadcasted_iota(jnp.int32, cos.shape, 1) % DIFF_HD
    half = DIFF_HD // 2

    def rope(x):
        partner = jnp.where(lane < half, pltpu.roll(x, LANES - half, 1), pltpu.roll(x, half, 1))
        return x * cos + partner * sin

    qt_ref[0] = (rope(q_ref[0]) * scale).T.astype(qt_ref.dtype)
    ko_ref[0] = rope(k_ref[0]).astype(ko_ref.dtype)
    vt_ref[0] = v_ref[0].T.astype(vt_ref.dtype)


def _diff_prep(ub, cos_t, sin_t, scale):
    b, n_all, _ = ub.shape
    tr = _pick(n_all, (768, 512, 384, 256, 128))
    col = lambda off: pl.BlockSpec((1, tr, LANES), lambda bi, i, h: (bi, i, off + h))
    tab = pl.BlockSpec((tr, LANES), lambda bi, i, h: (i, 0))
    feat = pl.BlockSpec((1, LANES, tr), lambda bi, i, h: (bi, h, i))
    return pl.pallas_call(
        functools.partial(_diff_prep_kernel, scale),
        grid=(b, n_all // tr, DIFF_HEADS),
        in_specs=[col(0), col(DIFF_HEADS), col(2 * DIFF_HEADS), tab, tab],
        out_specs=[feat, col(0), feat],
        out_shape=[jax.ShapeDtypeStruct((b, DIFF_W, n_all), BF16), jax.ShapeDtypeStruct((b, n_all, DIFF_W), BF16),
                   jax.ShapeDtypeStruct((b, DIFF_W, n_all), BF16)],
        compiler_params=pltpu.CompilerParams(dimension_semantics=("arbitrary",) * 3, vmem_limit_bytes=VMEM_LIMIT),
        name="diff_prep",
    )(ub, ub, ub, cos_t, sin_t)


SSD_GROUPS_PER_STEP = 8


def _ssd_scan_kernel(x_ref, b_ref, c_ref, dt_ref, bias_ref, alog_ref, y_ref, st_ref):
    t = SSD_CHUNK
    d = pl.program_id(1)
    gstep = pl.program_id(2)
    ngrp = x_ref.shape[2] // SSD_GW

    @pl.when(pl.program_id(3) == 0)
    def _():
        st_ref[...] = jnp.zeros_like(st_ref)

    dt = _softplus(dt_ref[0] + bias_ref[...])
    dta = dt * (-jnp.exp(alog_ref[...]))
    tr = lax.broadcasted_iota(jnp.int32, (t, t), 0)
    tc = lax.broadcasted_iota(jnp.int32, (t, t), 1)
    fwd = d == 0
    incl = jnp.where(fwd, tc - tr, tr - tc) <= 0
    x_all, b_all, c_all, st_all = x_ref[0], b_ref[0], c_ref[0], st_ref[...]
    outs = {}
    chains = []
    for j in range(ngrp):
        xs = slice(j * SSD_GW, (j + 1) * SSD_GW)
        ns = slice(j * SSD_STATE, (j + 1) * SSD_STATE)
        first_head = d * SSD_HEADS + (gstep * ngrp + j) * (SSD_HEADS // SSD_GROUPS)
        chains.append(_ssd_chain(x_all[:, xs], b_all[:, ns], c_all[:, ns], st_all[:, xs], dt, dta, first_head, fwd,
                                 incl, outs, j))
    while chains:
        alive = []
        for ch in chains:
            if next(ch, _DONE) is not _DONE:
                alive.append(ch)
        chains = alive
    for j in range(ngrp):
        xs = slice(j * SSD_GW, (j + 1) * SSD_GW)
        y, st_new = outs[j]
        y_ref[0, 0, :, xs] = y
        st_ref[:, xs] = st_new


def _ssd_chain(x, bm, cm, st, dt, dta, first_head, fwd, incl, outs, key):
    t = SSD_CHUNK
    e_heads = SSD_HEADS // SSD_GROUPS
    lane = lax.broadcasted_iota(jnp.int32, (t, 2 * SSD_HEADS), 1)
    lane_x = lax.broadcasted_iota(jnp.int32, (t, SSD_GW), 1) // SSD_HD
    lane_e = lax.broadcasted_iota(jnp.int32, (t, LANES), 1)
    dt_x = jnp.zeros((t, SSD_GW), F32)
    dta4 = jnp.zeros((t, LANES), F32)
    for e in range(e_heads):
        pick = lane == first_head + e
        dt_e = jnp.sum(jnp.where(pick, dt, 0.0), axis=1, keepdims=True)
        dta_e = jnp.sum(jnp.where(pick, dta, 0.0), axis=1, keepdims=True)
        dt_x = jnp.where(lane_x == e, dt_e, dt_x)
        dta4 = jnp.where(lane_e == e, dta_e, dta4)

    cum4 = _mm_sel(incl.astype(BF16), dta4)
    cb = _mm_nt(cm, bm)
    y_off_raw = _mm(cm, st)
    yield
    cum_t = cum4.T
    last4 = jnp.where(fwd, cum4[t - 1:t, :], cum4[0:1, :])
    cum_x = jnp.zeros((t, SSD_GW), F32)
    last_x = jnp.zeros((1, SSD_GW), F32)
    for e in range(e_heads):
        cum_x = jnp.where(lane_x == e, cum4[:, e:e + 1], cum_x)
        last_x = jnp.where(lane_x[:1] == e, last4[:, e:e + 1], last_x)

    xdt = x * dt_x
    zero_h = jnp.zeros((t, LANES), F32)
    lane_h = lax.broadcasted_iota(jnp.int32, (t, LANES), 1) < SSD_HD
    ydiag = []
    for pair in range(e_heads // 2):
        lmats = []
        for e in (2 * pair, 2 * pair + 1):
            seg = cum4[:, e:e + 1] - cum_t[e:e + 1, :]
            lmats.append(cb * jnp.exp(jnp.where(incl, seg, NEG_BIG)))
        xp = xdt[:, pair * LANES:(pair + 1) * LANES]
        xbd = jnp.concatenate([jnp.where(lane_h, xp, zero_h), jnp.where(lane_h, zero_h, xp)], axis=0)
        ydiag.append(_mm(jnp.concatenate(lmats, axis=1), xbd))
    states = _mm_tn(bm, xdt * jnp.exp(last_x - cum_x))
    yield
    y = jnp.concatenate(ydiag, axis=1) + y_off_raw * jnp.exp(cum_x)
    outs[key] = (y, st * jnp.exp(last_x) + states)


def _ssd_scan(xbc, dt_raw, dt_bias, a_log, n_ctx):
    b, n_all, _ = xbc.shape
    t = SSD_CHUNK
    nc = n_all // t
    ncc = n_ctx // t
    cidx = lambda d, i: _chunk_order(d, i, ncc, nc)
    gg = SSD_GROUPS_PER_STEP
    nsteps_g = SSD_GROUPS // gg
    xw, nw = gg * SSD_GW, gg * SSD_STATE
    b0 = SSD_W // nw
    row = pl.BlockSpec((1, 2 * SSD_HEADS), lambda bi, d, g, i: (0, 0))
    return pl.pallas_call(
        _ssd_scan_kernel,
        grid=(b, 2, nsteps_g, nc),
        in_specs=[pl.BlockSpec((1, t, xw), lambda bi, d, g, i: (bi, cidx(d, i), g)),
                  pl.BlockSpec((1, t, nw), lambda bi, d, g, i: (bi, cidx(d, i), b0 + g)),
                  pl.BlockSpec((1, t, nw), lambda bi, d, g, i: (bi, cidx(d, i), b0 + nsteps_g + g)),
                  pl.BlockSpec((1, t, 2 * SSD_HEADS), lambda bi, d, g, i: (bi, cidx(d, i), 0)),
                  row, row],
        out_specs=pl.BlockSpec((1, 1, t, xw), lambda bi, d, g, i: (bi, d, cidx(d, i), g)),
        out_shape=jax.ShapeDtypeStruct((b, 2, n_all, SSD_W), F32),
        scratch_shapes=[pltpu.VMEM((SSD_STATE, xw), F32)],
        compiler_params=pltpu.CompilerParams(dimension_semantics=("arbitrary",) * 4, vmem_limit_bytes=VMEM_LIMIT),
        name="ssd_scan",
    )(xbc, xbc, xbc, dt_raw, dt_bias.reshape(1, -1), a_log.reshape(1, -1))


def _rms_norm(x, g, eps=EPS):
    return x * lax.rsqrt(jnp.mean(x * x, axis=-1, keepdims=True) + eps) * g


HALO = 8


def _dwconv_kernel(n_ctx, n_all, silu, cur_ref, prev_ref, next_ref, w_ref, b_ref, o_ref):
    tr = cur_ref.shape[1]
    taps = w_ref.shape[0]
    pad = taps // 2
    ext = jnp.concatenate([prev_ref[0], cur_ref[0], next_ref[0]], axis=0)
    t = pl.program_id(1) * tr + lax.broadcasted_iota(jnp.int32, (tr, 1), 0)
    acc = cur_ref[0] * w_ref[pad:pad + 1, :]
    for o in range(-pad, pad + 1):
        if o == 0:
            continue
        ts = t + o
        lo, hi = (t, ts) if o > 0 else (ts, t)
        crosses = (lo < n_ctx) & (hi >= n_ctx)
        valid = (ts >= 0) & (ts < n_all) & jnp.logical_not(crosses)
        src = ext[HALO + o:HALO + o + tr]
        acc = acc + jnp.where(valid, src, 0.0) * w_ref[o + pad:o + pad + 1, :]
    if silu:
        acc = acc + b_ref[...]
        acc = acc * _sigmoid(acc)
    o_ref[0] = acc


def _dwconv(u, col0, ncols, w, bias, n_ctx, silu):
    b, n_all, _ = u.shape
    tr = _pick(n_all, (1408, 768, 512, 384, 256, 128))
    wt = _pick(math.gcd(ncols, col0) if col0 else ncols, (512, 256, 128))
    c0 = col0 // wt
    hb = tr // HALO
    last = n_all // HALO - 1
    if bias is None:
        bias = jnp.zeros((ncols,), F32)
    return pl.pallas_call(
        functools.partial(_dwconv_kernel, n_ctx, n_all, silu),
        grid=(b, n_all // tr, ncols // wt),
        in_specs=[pl.BlockSpec((1, tr, wt), lambda bi, i, j: (bi, i, c0 + j)),
                  pl.BlockSpec((1, HALO, wt), lambda bi, i, j: (bi, jnp.maximum(i * hb - 1, 0), c0 + j)),
                  pl.BlockSpec((1, HALO, wt), lambda bi, i, j: (bi, jnp.minimum((i + 1) * hb, last), c0 + j)),
                  pl.BlockSpec((w.shape[0], wt), lambda bi, i, j: (0, j)),
                  pl.BlockSpec((1, wt), lambda bi, i, j: (0, j))],
        out_specs=pl.BlockSpec((1, tr, wt), lambda bi, i, j: (bi, i, j)),
        out_shape=jax.ShapeDtypeStruct((b, n_all, ncols), F32),
        compiler_params=pltpu.CompilerParams(dimension_semantics=("arbitrary",) * 3, vmem_limit_bytes=VMEM_LIMIT),
        name="dwconv",
    )(u, u, u, w, bias.reshape(1, ncols))


def _row_select(tbl_ref, i, tm, rows_per_batch, n_ctx):
    tiles_per_batch = rows_per_batch // tm
    bi = i // tiles_per_batch
    row = (i % tiles_per_batch) * tm + lax.broadcasted_iota(jnp.int32, (tm, 1), 0)
    nb = tbl_ref.shape[0] - 1
    return jnp.where(row < n_ctx, tbl_ref[nb:nb + 1, :], tbl_ref[pl.ds(bi, 1), :])


def _prenorm_kernel(n_ctx, rows_per_batch, x_ref, g_ref, shift_ref, scale_ref, o_ref):
    i = pl.program_id(0)
    tm = x_ref.shape[0]
    x = x_ref[...]
    y = x * lax.rsqrt(jnp.mean(x * x, axis=-1, keepdims=True) + EPS) * g_ref[...]
    scale = _row_select(scale_ref, i, tm, rows_per_batch, n_ctx)
    shift = _row_select(shift_ref, i, tm, rows_per_batch, n_ctx)
    o_ref[...] = (y * (1.0 + scale) + shift).astype(o_ref.dtype)


def _prenorm(x2d, g, shift_tbl, scale_tbl, n_ctx, rows_per_batch):
    m, n = x2d.shape
    tm = _pick(rows_per_batch, (384, 256, 128))
    tbl = pl.BlockSpec(shift_tbl.shape, lambda i: (0, 0))
    return pl.pallas_call(
        functools.partial(_prenorm_kernel, n_ctx, rows_per_batch),
        grid=(m // tm,),
        in_specs=[pl.BlockSpec((tm, n), lambda i: (i, 0)), pl.BlockSpec((1, n), lambda i: (0, 0)), tbl, tbl],
        out_specs=pl.BlockSpec((tm, n), lambda i: (i, 0)),
        out_shape=jax.ShapeDtypeStruct((m, n), BF16),
        compiler_params=pltpu.CompilerParams(dimension_semantics=("arbitrary",), vmem_limit_bytes=VMEM_LIMIT),
        name="prenorm",
    )(x2d, g.reshape(1, n), shift_tbl, scale_tbl)


def _rwkv_finish_kernel(yf_ref, yb_ref, bonus_ref, g_ref, w_ref, b_ref, o_ref):
    y = yf_ref[0] + yb_ref[0]
    hr = lax.broadcasted_iota(jnp.int32, (LANES, LANES), 0) // RWKV_HD
    hc = lax.broadcasted_iota(jnp.int32, (LANES, LANES), 1) // RWKV_HD
    head_ones = (hr == hc).astype(BF16)
    outs = []
    for p in range(y.shape[1] // LANES):
        yp = y[:, p * LANES:(p + 1) * LANES]
        mu = _mm_sel_right(yp, head_ones) * (1.0 / RWKV_HD)
        dev = yp - mu
        var = _mm_sel_right(dev * dev, head_ones) * (1.0 / RWKV_HD)
        outs.append(dev * lax.rsqrt(var + RWKV_GN_EPS))
    yn = jnp.concatenate(outs, axis=1) * w_ref[...] + b_ref[...]
    g = g_ref[0]
    o_ref[0] = ((yn + bonus_ref[0]) * (g * _sigmoid(g))).astype(o_ref.dtype)


def _rwkv_finish(yf, yb, bonus, ua, g_col0, lnx_w, lnx_b):
    b, n_all, _ = yf.shape
    tr = _pick(n_all, (768, 512, 384, 256, 128))
    wt = 256
    g0 = g_col0 // wt
    vec = pl.BlockSpec((1, wt), lambda bi, i, j: (0, j))
    blk = lambda c0: pl.BlockSpec((1, tr, wt), lambda bi, i, j: (bi, i, c0 + j))
    return pl.pallas_call(
        _rwkv_finish_kernel,
        grid=(b, n_all // tr, RWKV_W // wt),
        in_specs=[blk(0), blk(0), blk(0), blk(g0), vec, vec],
        out_specs=blk(0),
        out_shape=jax.ShapeDtypeStruct((b, n_all, RWKV_W), BF16),
        compiler_params=pltpu.CompilerParams(dimension_semantics=("arbitrary",) * 3, vmem_limit_bytes=VMEM_LIMIT),
        name="rwkv_finish",
    )(yf, yb, bonus, ua, lnx_w.reshape(1, -1), lnx_b.reshape(1, -1))


def _ssd_finish_kernel(y_ref, x_ref, z_ref, d_ref, g_ref, o_ref):
    z = z_ref[0]
    y = (y_ref[0, 0] + y_ref[0, 1] + d_ref[...] * x_ref[0]) * (z * _sigmoid(z))
    outs = []
    for grp in range(y.shape[1] // SSD_GW):
        yg = y[:, grp * SSD_GW:(grp + 1) * SSD_GW]
        outs.append(yg * lax.rsqrt(jnp.mean(yg * yg, axis=-1, keepdims=True) + EPS))
    o_ref[0] = (jnp.concatenate(outs, axis=1) * g_ref[...]).astype(o_ref.dtype)


def _ssd_finish(y2, xact, uc, d_vec, norm_g):
    b, _, n_all, _ = y2.shape
    tr = _pick(n_all, (384, 256, 128))
    wt = 1024
    vec = pl.BlockSpec((1, wt), lambda bi, i, j: (0, j))
    blk = pl.BlockSpec((1, tr, wt), lambda bi, i, j: (bi, i, j))
    return pl.pallas_call(
        _ssd_finish_kernel,
        grid=(b, n_all // tr, SSD_W // wt),
        in_specs=[pl.BlockSpec((1, 2, tr, wt), lambda bi, i, j: (bi, 0, i, j)), blk, blk, vec, vec],
        out_specs=blk,
        out_shape=jax.ShapeDtypeStruct((b, n_all, SSD_W), BF16),
        compiler_params=pltpu.CompilerParams(dimension_semantics=("arbitrary",) * 3, vmem_limit_bytes=VMEM_LIMIT),
        name="ssd_finish",
    )(y2, xact, uc, d_vec.reshape(1, -1), norm_g.reshape(1, -1))


def _axial_rope(n_tok):
    rows = n_tok // GRID_W
    row = jnp.repeat(jnp.arange(rows, dtype=F32), GRID_W)
    col = jnp.tile(jnp.arange(GRID_W, dtype=F32), rows)
    n_freq = DIFF_HD // 4
    inv = ROPE_BASE ** (-jnp.arange(n_freq, dtype=F32) / n_freq)
    ang = jnp.concatenate([row[:, None] * inv, col[:, None] * inv], axis=-1)
    return jnp.cos(ang), jnp.sin(ang)


def _rwkv_branch(ua, n_ctx, conv_w, w0, w2, a0, a2, k_k, k_a, r_k, lnx_w, lnx_b):
    uconv = _dwconv(ua, 0, RWKV_CONV_CH, conv_w, None, n_ctx, False)
    m, n, q, yl, bonus = _rwkv_local(uconv, w0, w2, a0, a2, k_k, k_a, r_k)
    yf, yb = _rwkv_scan(m, n, q, yl, n_ctx)
    return _rwkv_finish(yf, yb, bonus, ua, RWKV_CONV_CH, lnx_w, lnx_b)


def _diff_branch(ub, n_ctx, cos_t, sin_t, lam_p, subln_g, lam_init):
    lam = jnp.exp(jnp.sum(lam_p[0] * lam_p[1])) - jnp.exp(jnp.sum(lam_p[2] * lam_p[3])) + lam_init
    scale = DIFF_HD ** -0.5 * math.log2(math.e)
    qt, k_all, vt = _diff_prep(ub, cos_t, sin_t, scale)
    g = ub[..., 3 * DIFF_W:]
    post = 1.0 - lam_init
    oc = _diff_attn(qt[:, :, :n_ctx], k_all[:, :n_ctx], vt[:, :, :n_ctx], g[:, :n_ctx], subln_g, lam, post)
    ol = _diff_attn(qt[:, :, n_ctx:], k_all, vt, g[:, n_ctx:], subln_g, lam, post)
    return jnp.concatenate([oc, ol], axis=1)


def _rope_tables(n_ctx, n_lat):
    cos, sin = _axial_rope(n_lat)
    cos4 = jnp.tile(cos, (1, 4))
    sin4 = jnp.tile(jnp.concatenate([-sin, sin], axis=1), (1, 2))
    ident = lambda v: jnp.full((n_ctx, LANES), v, F32)
    return jnp.concatenate([ident(1.0), cos4], axis=0), jnp.concatenate([ident(0.0), sin4], axis=0)


def _ssd_branch(uc, dt_raw, n_ctx, conv_w, conv_b, a_log, dt_bias, d_skip, norm_g):
    xact = _dwconv(uc, SSD_W, SSD_XBC, conv_w, conv_b, n_ctx, True)
    y2 = _ssd_scan(xact, dt_raw, dt_bias, a_log, n_ctx)
    return _ssd_finish(y2, xact, uc, jnp.repeat(d_skip, SSD_HD), norm_g)


def kernel(x, c, ctx, c_ctx, w_mod, b_mod, g_pre, g_post, w_in, rwkv_conv, rwkv_w0, rwkv_w2, rwkv_a0, rwkv_a2,
           rwkv_k_k, rwkv_k_a, rwkv_r_k, rwkv_lnx_w, rwkv_lnx_b, diff_lambda, diff_subln, ssd_conv_w, ssd_conv_b,
           ssd_a_log, ssd_dt_bias, ssd_d, ssd_norm, w_branch_a, w_branch_b, w_branch_c, w_out):
    b, n_lat, dm = x.shape
    n_ctx = ctx.shape[1]
    n_all = n_ctx + n_lat
    depth = w_in.shape[0]
    cos_t, sin_t = _rope_tables(n_ctx, n_lat)
    cond = jax.nn.silu(jnp.concatenate([c, c_ctx[None, :]], axis=0))
    cond = jnp.pad(cond, ((0, 8 - (b + 1)), (0, 0)))

    o_rw, o_g, o_dt = 0, RWKV_CONV_CH, RWKV_CONV_CH + RWKV_W
    src_dt = o_dt + 4 * DIFF_W + SSD_W + SSD_XBC
    pad_a = -(o_dt + 2 * SSD_HEADS) % 512

    xa = jnp.concatenate([ctx, x], axis=1)
    for li in range(depth):
        lam_init = 0.8 - 0.6 * math.exp(-0.3 * li)
        mod = _matmul(cond, w_mod, tn_cands=(1024, 512, 256, 128), layer=li)[:b + 1] + b_mod[li]
        shift_l, scale_l, gate_l = jnp.split(mod[:b], 3, axis=-1)
        shift_c, scale_c, gate_c = jnp.split(mod[b], 3, axis=-1)
        tbl = lambda vl, vc: jnp.concatenate([vl, vc[None, :]], axis=0)
        hb = _prenorm(xa.reshape(b * n_all, dm), g_pre[li], tbl(shift_l, shift_c), tbl(scale_l, scale_c), n_ctx,
                      n_all)
        wl = w_in[li]
        proj = lambda w: _matmul(hb, w.astype(BF16), tm_cands=(1536, 768, 512, 384, 256, 128)).reshape(b, n_all, -1)
        ua = proj(jnp.concatenate([wl[:, :o_dt], wl[:, src_dt:src_dt + 2 * SSD_HEADS],
                                   jnp.zeros((dm, pad_a), wl.dtype)], axis=1))
        ub = proj(wl[:, o_dt:o_dt + 4 * DIFF_W])
        uc = proj(wl[:, o_dt + 4 * DIFF_W:src_dt])
        ug = proj(wl[:, src_dt + 2 * SSD_HEADS:])

        oa = _rwkv_branch(ua, n_ctx, rwkv_conv[li], rwkv_w0[li], rwkv_w2[li], rwkv_a0[li], rwkv_a2[li],
                          rwkv_k_k[li], rwkv_k_a[li], rwkv_r_k[li], rwkv_lnx_w[li], rwkv_lnx_b[li])
        ob = _diff_branch(ub, n_ctx, cos_t, sin_t, diff_lambda[li], diff_subln[li], lam_init)
        oc = _ssd_branch(uc, ua[..., o_dt:o_dt + 2 * SSD_HEADS], n_ctx, ssd_conv_w[li], ssd_conv_b[li],
                         ssd_a_log[li], ssd_dt_bias[li], ssd_d[li], ssd_norm[li])

        flat = lambda t: t.astype(BF16).reshape(b * n_all, -1)
        mrg = _merge(flat(oa), flat(ob), flat(oc), w_branch_a[li].astype(BF16), w_branch_b[li].astype(BF16),
                     w_branch_c[li].astype(BF16), ug.reshape(b * n_all, 3 * dm))
        xa = _out_norm(mrg, w_out[li].astype(BF16), xa.reshape(b * n_all, dm), tbl(gate_l, gate_c), g_post[li],
                       n_ctx, n_all).reshape(b, n_all, dm)
    return xa[:, n_ctx:]
```

```python
import functools
import math

import jax
import jax.numpy as jnp
from jax import lax
from jax.experimental import pallas as pl
from jax.experimental.pallas import tpu as pltpu

F32 = jnp.float32
BF16 = jnp.bfloat16

EPS = 1e-6
GRID_W = 64
ROPE_BASE = 10000.0

RWKV_HEADS = 16
RWKV_HD = 64
RWKV_W = RWKV_HEADS * RWKV_HD
RWKV_LORA = 64
RWKV_CONV_CH = 3 * RWKV_W + 4 * RWKV_LORA
RWKV_GN_EPS = 64e-5
RWKV_CHUNK = 64
RWKV_LOCAL_WIDTH = 1024

DIFF_HEADS = 8
DIFF_HD = 64
DIFF_VD = 2 * DIFF_HD
DIFF_W = DIFF_HEADS * DIFF_VD

SSD_HEADS = 32
SSD_HD = 64
SSD_W = SSD_HEADS * SSD_HD
SSD_GROUPS = 8
SSD_STATE = 128
SSD_CHUNK = 128
SSD_XBC = SSD_W + 2 * SSD_GROUPS * SSD_STATE
SSD_GW = SSD_W // SSD_GROUPS

LANES = 128
VMEM_LIMIT = 56 * 1024 * 1024

NEG_BIG = -1e30


def _pick(n, candidates):
    for c in candidates:
        if n % c == 0:
            return c
    return n


def _mm(a, b):
    return jnp.dot(a.astype(BF16), b.astype(BF16), preferred_element_type=F32)


def _mm_nt(a, b):
    return lax.dot_general(a.astype(BF16), b.astype(BF16), (((1,), (1,)), ((), ())), preferred_element_type=F32)


def _mm_tn(a, b):
    return lax.dot_general(a.astype(BF16), b.astype(BF16), (((0,), (0,)), ((), ())), preferred_element_type=F32)


def _split3(x):
    x1 = x.astype(BF16)
    r1 = x - x1.astype(F32)
    x2 = r1.astype(BF16)
    x3 = (r1 - x2.astype(F32)).astype(BF16)
    return x1, x2, x3


def _mm_x3(a, b):
    a1 = a.astype(BF16)
    a2 = (a - a1.astype(F32)).astype(BF16)
    b1 = b.astype(BF16)
    b2 = (b - b1.astype(F32)).astype(BF16)
    d = lambda x, y: jnp.dot(x, y, preferred_element_type=F32)
    return d(a1, b1) + (d(a1, b2) + d(a2, b1))


def _mm_sel(sel, x):
    x1, x2, x3 = _split3(x)
    d = lambda y: jnp.dot(sel, y, preferred_element_type=F32)
    return d(x1) + d(x2) + d(x3)


def _softplus(x):
    return jnp.maximum(x, 0.0) + jnp.log(1.0 + jnp.exp(-jnp.abs(x)))


def _sigmoid(x):
    return 1.0 / (1.0 + jnp.exp(-x))


def _matmul_kernel(a_ref, w_ref, o_ref):
    o_ref[...] = _mm(a_ref[...], w_ref[...]).astype(o_ref.dtype)


def _matmul(a, w, out_dtype=F32, tm_cands=(768, 512, 384, 256, 128, 8), tn_cands=(512, 256, 128), layer=None):
    m, k = a.shape
    n = w.shape[-1]
    tm = _pick(m, tm_cands)
    tn = _pick(n, tn_cands)
    if layer is None:
        w_spec = pl.BlockSpec((k, tn), lambda i, j: (0, j))
    else:
        w_spec = pl.BlockSpec((None, k, tn), lambda i, j: (layer, 0, j))
    return pl.pallas_call(
        _matmul_kernel,
        grid=(m // tm, n // tn),
        in_specs=[pl.BlockSpec((tm, k), lambda i, j: (i, 0)), w_spec],
        out_specs=pl.BlockSpec((tm, tn), lambda i, j: (i, j)),
        out_shape=jax.ShapeDtypeStruct((m, n), out_dtype),
        compiler_params=pltpu.CompilerParams(dimension_semantics=("arbitrary", "arbitrary"),
                                             vmem_limit_bytes=VMEM_LIMIT),
        name="dense_matmul",
    )(a, w)


def _merge_kernel(oa_ref, ob_ref, oc_ref, wa_ref, wb_ref, wc_ref, sa_ref, sb_ref, sc_ref, o_ref):
    dot = lambda x_ref, w_ref: jnp.dot(x_ref[...], w_ref[...], preferred_element_type=F32)
    o = (_sigmoid(sa_ref[...]) * dot(oa_ref, wa_ref) + _sigmoid(sb_ref[...]) * dot(ob_ref, wb_ref)
         + _sigmoid(sc_ref[...]) * dot(oc_ref, wc_ref))
    o_ref[...] = o.astype(o_ref.dtype)


def _merge(oa, ob, oc, wa, wb, wc, gates):
    m = oa.shape[0]
    n = wa.shape[1]
    tm = _pick(m, (768, 512, 384, 256, 128))
    tn = _pick(n, (512, 256, 128))
    nj = n // tn
    act = lambda t: pl.BlockSpec((tm, t.shape[1]), lambda i, j: (i, 0))
    wgt = lambda t: pl.BlockSpec((t.shape[0], tn), lambda i, j: (0, j))
    gate = lambda s: pl.BlockSpec((tm, tn), lambda i, j: (i, s * nj + j))
    return pl.pallas_call(
        _merge_kernel,
        grid=(m // tm, nj),
        in_specs=[act(oa), act(ob), act(oc), wgt(wa), wgt(wb), wgt(wc), gate(0), gate(1), gate(2)],
        out_specs=pl.BlockSpec((tm, tn), lambda i, j: (i, j)),
        out_shape=jax.ShapeDtypeStruct((m, n), BF16),
        compiler_params=pltpu.CompilerParams(dimension_semantics=("arbitrary", "arbitrary"),
                                             vmem_limit_bytes=VMEM_LIMIT),
        name="branch_merge",
    )(oa, ob, oc, wa, wb, wc, gates, gates, gates)


def _out_norm_kernel(n_ctx, rows_per_batch, m_ref, w_ref, x_ref, gate_ref, g_ref, o_ref, acc_ref):
    i = pl.program_id(0)
    kk = pl.program_id(1)

    @pl.when(kk == 0)
    def _():
        acc_ref[...] = jnp.zeros_like(acc_ref)

    acc_ref[...] += jnp.dot(m_ref[...], w_ref[...], preferred_element_type=F32)

    @pl.when(kk == pl.num_programs(1) - 1)
    def _():
        tm = acc_ref.shape[0]
        out = acc_ref[...]
        y = out * lax.rsqrt(jnp.mean(out * out, axis=-1, keepdims=True) + EPS) * g_ref[...]
        tiles_per_batch = rows_per_batch // tm
        bi = i // tiles_per_batch
        row = (i % tiles_per_batch) * tm + lax.broadcasted_iota(jnp.int32, (tm, 1), 0)
        nb = gate_ref.shape[0] - 1
        gate = jnp.where(row < n_ctx, gate_ref[nb:nb + 1, :], gate_ref[pl.ds(bi, 1), :])
        o_ref[...] = x_ref[...] + gate * y


def _out_norm(mrg, w, x2d, gates, g_post, n_ctx, rows_per_batch):
    m, k = mrg.shape
    n = w.shape[1]
    tm = _pick(rows_per_batch, (384, 256, 128))
    tk = _pick(k, (512, 256, 128))
    return pl.pallas_call(
        functools.partial(_out_norm_kernel, n_ctx, rows_per_batch),
        grid=(m // tm, k // tk),
        in_specs=[pl.BlockSpec((tm, tk), lambda i, kk: (i, kk)),
                  pl.BlockSpec((tk, n), lambda i, kk: (kk, 0)),
                  pl.BlockSpec((tm, n), lambda i, kk: (i, 0)),
                  pl.BlockSpec(gates.shape, lambda i, kk: (0, 0)),
                  pl.BlockSpec((1, n), lambda i, kk: (0, 0))],
        out_specs=pl.BlockSpec((tm, n), lambda i, kk: (i, 0)),
        out_shape=jax.ShapeDtypeStruct((m, n), F32),
        scratch_shapes=[pltpu.VMEM((tm, n), F32)],
        compiler_params=pltpu.CompilerParams(dimension_semantics=("arbitrary", "arbitrary"),
                                             vmem_limit_bytes=VMEM_LIMIT),
        name="out_norm",
    )(mrg, w, x2d, gates, g_post.reshape(1, n))


def _head_masks(rows):
    lane = lax.broadcasted_iota(jnp.int32, (rows, LANES), 1)
    return lane < RWKV_HD


def _bd(x, first):
    zero = jnp.zeros_like(x)
    return jnp.concatenate([jnp.where(first, x, zero), jnp.where(first, zero, x)], axis=0)


def _rwkv_local_kernel(r_ref, k_ref, v_ref, lw_ref, la_ref, w0_ref, w2_ref, a0_ref, a2_ref, kk_ref, ka_ref,
                       rk_ref, m_ref, n_ref, q_ref, yl_ref, bonus_ref):
    c = RWKV_CHUNK
    width = r_ref.shape[2]
    first = _head_masks(c)
    row = lax.broadcasted_iota(jnp.int32, (c, LANES), 0)
    col = lax.broadcasted_iota(jnp.int32, (c, LANES), 1) % RWKV_HD
    eye_cat = (row == col).astype(F32)
    hr = lax.broadcasted_iota(jnp.int32, (LANES, LANES), 0) // RWKV_HD
    hc = lax.broadcasted_iota(jnp.int32, (LANES, LANES), 1) // RWKV_HD
    head_ones = (hr == hc).astype(BF16)
    tr = lax.broadcasted_iota(jnp.int32, (c, c), 0)
    tc = lax.broadcasted_iota(jnp.int32, (c, c), 1)
    zrow = jnp.zeros((RWKV_LORA, width), F32)

    tanh_lw = jnp.tanh(lw_ref[0])
    la = la_ref[0]
    w_raw2, a_sig2 = [], []
    for d in (0, 1):
        pad = (lambda m: jnp.concatenate([m, zrow], axis=0)) if d == 0 else (lambda m: jnp.concatenate([zrow, m], axis=0))
        w_raw2.append(w0_ref[d:d + 1, :] + _mm(tanh_lw, pad(w2_ref[d])))
        a_sig2.append(_sigmoid(a0_ref[d:d + 1, :] + _mm(la, pad(a2_ref[d]))))

    r_all, k_all, v_all = r_ref[0], k_ref[0], v_ref[0]
    kk_all, ka_all, rk_all = kk_ref[...], ka_ref[...], rk_ref[...]
    consts = (first, row, col, eye_cat, tr, tc)
    chains, outs = [], {}
    for pi in range(width // LANES):
        sl = slice(pi * LANES, (pi + 1) * LANES)
        r, k, v = r_all[:, sl], k_all[:, sl], v_all[:, sl]
        kkp = k * kk_all[:, sl]
        kk = kkp / jnp.maximum(jnp.sqrt(_mm_sel_right(kkp * kkp, head_ones)), 1e-12)
        a_sum = a_sig2[0][:, sl] + a_sig2[1][:, sl]
        kd_sum = k * (2.0 + (a_sum - 2.0) * ka_all[:, sl])
        outs[pi] = _mm_sel_right(r * kd_sum * rk_all[:, sl], head_ones) * v
        for d in (0, 1):
            chains.append(_rwkv_chain(d, r, k, v, kk, w_raw2[d][:, sl], a_sig2[d][:, sl], ka_all[:, sl], consts,
                                      outs, (pi, d)))
    while chains:
        alive = []
        for ch in chains:
            if next(ch, _DONE) is not _DONE:
                alive.append(ch)
        chains = alive
    for pi in range(width // LANES):
        sl = slice(pi * LANES, (pi + 1) * LANES)
        for d in (0, 1):
            m_cat, n_cat, q, yloc = outs[(pi, d)]
            m_ref[0, d, 0, :, sl] = m_cat
            n_ref[0, d, 0, :, sl] = n_cat
            q_ref[0, d, :, sl] = q
            yl_ref[0, d, :, sl] = yloc
        bonus_ref[0, :, sl] = outs[pi]


_DONE = object()


def _rwkv_chain(d, r, k, v, kk, w_raw, a_sig, ka, consts, outs, key):
    c = RWKV_CHUNK
    first, row, col, eye_cat, tr, tc = consts
    if d == 0:
        incl, strict, tri = col <= row, col < row, (tc <= tr)
    else:
        incl, strict, tri = col >= row, col > row, (tc >= tr)
    logw = -jnp.exp(-_softplus(-w_raw) - 0.5)
    k_dir = k * (1.0 + (a_sig - 1.0) * ka)
    b_dir = kk * a_sig

    cum = _mm_sel(tri.astype(BF16), logw)
    yield
    tot = cum[c - 1:c, :] if d == 0 else cum[0:1, :]
    g_incl = jnp.exp(cum)
    g_excl = jnp.exp(cum - logw)
    g_inv = jnp.exp(-cum)
    g_hat = jnp.exp(tot - cum)
    at = -kk * g_excl
    rt = r * g_incl
    bt = b_dir * g_inv
    kt = k_dir * g_inv
    bh = b_dir * g_hat
    kh = k_dir * g_hat

    sc = _mm_nt(jnp.concatenate([at, rt], axis=0),
                jnp.concatenate([_bd(bt, first), _bd(kt, first)], axis=0))
    yield
    zero = jnp.zeros((c, LANES), F32)
    a_ab = jnp.where(strict, sc[:c, :LANES], zero)
    a_ak = jnp.where(strict, sc[:c, LANES:], zero)
    a_rb = jnp.where(incl, sc[c:, :LANES], zero)
    a_rk = jnp.where(incl, sc[c:, LANES:], zero)

    t = eye_cat + a_ab
    p = _mm(a_ab, _bd(a_ab, first))
    akv = _mm(a_ak, _bd(v, first))
    yield
    for _ in range(4):
        pbd = _bd(p, first)
        t, p = t + _mm(t, pbd), _mm(p, pbd)
        yield
    t = t + _mm(t, _bd(p, first))
    yield
    resid = eye_cat - t + _mm_x3(a_ab, _bd(t, first))
    yield
    t = t + _mm(t, _bd(resid, first))
    yield

    x = _mm(t, jnp.concatenate([_bd(akv, first), _bd(at, first)], axis=1))
    yield
    uloc = x[:, :LANES]
    w = x[:, LANES:]
    yloc = _mm(jnp.concatenate([a_rb, a_rk], axis=1),
               jnp.concatenate([_bd(uloc, first), _bd(v, first)], axis=0))
    q = rt + _mm(a_rb, _bd(w, first))
    g = _mm_tn(jnp.concatenate([bh, kh], axis=1),
               jnp.concatenate([uloc, w, v], axis=1))
    yield
    nfull = g[:LANES, :LANES] + g[LANES:, 2 * LANES:]
    mfull = g[:LANES, LANES:2 * LANES]
    n_cat = jnp.where(first, nfull[:c], zero) + jnp.where(first, zero, nfull[c:])
    m_cat = jnp.where(first, mfull[:c], zero) + jnp.where(first, zero, mfull[c:]) + eye_cat * jnp.exp(tot)
    outs[key] = (m_cat, n_cat, q, yloc)


def _mm_sel_right(x, sel):
    x1, x2, x3 = _split3(x)
    d = lambda y: jnp.dot(y, sel, preferred_element_type=F32)
    return d(x1) + d(x2) + d(x3)


def _rwkv_local(uconv, w0, w2, a0, a2, k_k, k_a, r_k):
    b, n_all, _ = uconv.shape
    c = RWKV_CHUNK
    nc = n_all // c
    wb = RWKV_LOCAL_WIDTH
    nblk = RWKV_W // wb
    row = lambda t: t.reshape(1, RWKV_W)
    col_blk = lambda off: pl.BlockSpec((1, c, wb), lambda bi, ci, pi: (bi, ci, off + pi))
    lora_blk = lambda off: pl.BlockSpec((1, c, LANES), lambda bi, ci, pi: (bi, ci, off))
    vec2 = pl.BlockSpec((2, wb), lambda bi, ci, pi: (0, pi))
    mat2 = pl.BlockSpec((2, RWKV_LORA, wb), lambda bi, ci, pi: (0, 0, pi))
    vec1 = pl.BlockSpec((1, wb), lambda bi, ci, pi: (0, pi))
    mn_shape = jax.ShapeDtypeStruct((b, 2, nc, RWKV_HD, RWKV_W), F32)
    mn_spec = pl.BlockSpec((1, 2, 1, RWKV_HD, wb), lambda bi, ci, pi: (bi, 0, ci, 0, pi))
    qy_shape = jax.ShapeDtypeStruct((b, 2, n_all, RWKV_W), F32)
    qy_spec = pl.BlockSpec((1, 2, c, wb), lambda bi, ci, pi: (bi, 0, ci, pi))
    lora0 = 3 * RWKV_W // LANES
    return pl.pallas_call(
        _rwkv_local_kernel,
        grid=(b, nc, nblk),
        in_specs=[col_blk(0), col_blk(nblk), col_blk(2 * nblk), lora_blk(lora0), lora_blk(lora0 + 1),
                  vec2, mat2, vec2, mat2, vec1, vec1, vec1],
        out_specs=[mn_spec, mn_spec, qy_spec, qy_spec,
                   pl.BlockSpec((1, c, wb), lambda bi, ci, pi: (bi, ci, pi))],
        out_shape=[mn_shape, mn_shape, qy_shape, qy_shape, jax.ShapeDtypeStruct((b, n_all, RWKV_W), F32)],
        compiler_params=pltpu.CompilerParams(dimension_semantics=("arbitrary",) * 3, vmem_limit_bytes=VMEM_LIMIT),
        name="rwkv_local",
    )(uconv, uconv, uconv, uconv, uconv, w0, w2, a0, a2, row(k_k), row(k_a), row(r_k))


def _rwkv_scan_kernel(mf_ref, nf_ref, qf_ref, ylf_ref, mb_ref, nb_ref, qb_ref, ylb_ref, yf_ref, yb_ref, s_ref):
    c = RWKV_CHUNK

    @pl.when(pl.program_id(1) == 0)
    def _():
        s_ref[...] = jnp.zeros_like(s_ref)

    first = _head_masks(RWKV_HD)
    dot = lambda x, y: jnp.dot(x, y, preferred_element_type=F32)
    for d, (m_ref, n_ref, q_ref, yl_ref, y_ref) in enumerate(((mf_ref, nf_ref, qf_ref, ylf_ref, yf_ref),
                                                             (mb_ref, nb_ref, qb_ref, ylb_ref, yb_ref))):
        for p in range(RWKV_W // LANES):
            sl = slice(p * LANES, (p + 1) * LANES)
            s1, s2, s3 = _split3(_bd(s_ref[d, :, sl], first))
            lhs = jnp.concatenate([q_ref[0, 0, :, sl], m_ref[0, 0, 0, :, sl]], axis=0)
            l1, l2, l3 = _split3(lhs)
            z = dot(l1, s1) + (dot(l1, s2) + dot(l2, s1)) + (dot(l2, s2) + dot(l1, s3) + dot(l3, s1))
            y_ref[0, :, sl] = yl_ref[0, 0, :, sl] + z[:c]
            s_ref[d, :, sl] = z[c:] + n_ref[0, 0, 0, :, sl]


def _chunk_order(d, i, n_ctx_chunks, n_chunks):
    back = jnp.where(i < n_ctx_chunks, n_ctx_chunks - 1 - i, n_chunks + n_ctx_chunks - 1 - i)
    return jnp.where(d == 0, i, back)


def _rwkv_scan(m, n, q, yl, n_ctx):
    b, _, nc, _, _ = m.shape
    n_all = q.shape[2]
    c = RWKV_CHUNK
    ncc = n_ctx // c
    cidx = lambda d, i: _chunk_order(d, i, ncc, nc)
    mn_spec = lambda d: pl.BlockSpec((1, 1, 1, RWKV_HD, RWKV_W), lambda bi, i: (bi, d, cidx(d, i), 0, 0))
    qy_spec = lambda d: pl.BlockSpec((1, 1, c, RWKV_W), lambda bi, i: (bi, d, cidx(d, i), 0))
    y_spec = lambda d: pl.BlockSpec((1, c, RWKV_W), lambda bi, i: (bi, cidx(d, i), 0))
    y_shape = jax.ShapeDtypeStruct((b, n_all, RWKV_W), F32)
    return pl.pallas_call(
        _rwkv_scan_kernel,
        grid=(b, nc),
        in_specs=[mn_spec(0), mn_spec(0), qy_spec(0), qy_spec(0), mn_spec(1), mn_spec(1), qy_spec(1), qy_spec(1)],
        out_specs=[y_spec(0), y_spec(1)],
        out_shape=[y_shape, y_shape],
        scratch_shapes=[pltpu.VMEM((2, RWKV_HD, RWKV_W), F32)],
        compiler_params=pltpu.CompilerParams(dimension_semantics=("arbitrary",) * 2, vmem_limit_bytes=VMEM_LIMIT),
        name="rwkv_scan",
    )(m, n, q, yl, m, n, q, yl)


ATTN_COLS = 256


def _diff_attn_kernel(tsub, lam_ref, qt_ref, k_ref, vt_ref, g_ref, subln_ref, o_ref, qs_ref, m_ref, l_ref, acc_ref):
    tq = qt_ref.shape[2]
    tk = k_ref.shape[1]
    kv = pl.program_id(3)

    @pl.when(kv == 0)
    def _():
        qt = qt_ref[0]
        feat = lax.broadcasted_iota(jnp.int32, qt.shape, 0)
        zero = jnp.zeros_like(qt)
        qs_ref[...] = jnp.concatenate([jnp.where(feat < DIFF_HD, qt, zero), jnp.where(feat < DIFF_HD, zero, qt)], axis=1)
        m_ref[...] = jnp.full_like(m_ref, NEG_BIG)
        l_ref[...] = jnp.zeros_like(l_ref)
        acc_ref[...] = jnp.zeros_like(acc_ref)

    nsub = tk // tsub
    strips = [slice(cb * ATTN_COLS, (cb + 1) * ATTN_COLS) for cb in range(2 * tq // ATTN_COLS)]
    ones = jnp.ones((ATTN_ONES_ROWS, tsub), BF16)

    def scores_of(j):
        kj = k_ref[0, j * tsub:(j + 1) * tsub, :]
        return [jnp.dot(kj, qs_ref[:, sl], preferred_element_type=F32) for sl in strips]

    nxt = scores_of(0)
    for j in range(nsub):
        scores = nxt
        if j + 1 < nsub:
            nxt = scores_of(j + 1)
        v_ext = jnp.concatenate([vt_ref[0, :, j * tsub:(j + 1) * tsub], ones], axis=0)
        for sl, s in zip(strips, scores):
            m_old = m_ref[:, sl]
            m_new = jnp.maximum(m_old, jnp.max(s, axis=0, keepdims=True))
            alpha = jnp.exp2(m_old - m_new)
            p = jnp.exp2(s - m_new[:1]).astype(BF16)
            pv = jnp.dot(v_ext, p, preferred_element_type=F32)
            acc_ref[:, sl] = alpha[:1] * acc_ref[:, sl] + pv[:DIFF_VD]
            l_ref[:, sl] = alpha * l_ref[:, sl] + pv[DIFF_VD:DIFF_VD + 8]
            m_ref[:, sl] = m_new

    @pl.when(kv == pl.num_programs(3) - 1)
    def _():
        o = acc_ref[...] / l_ref[:1, :]
        diff = o[:, :tq] - lam_ref[0, 0] * o[:, tq:]
        inv = lax.rsqrt(jnp.mean(diff * diff, axis=0, keepdims=True) + EPS)
        y = (diff * inv * (subln_ref[...] * lam_ref[0, 1])).T
        g = g_ref[0]
        o_ref[0] = (y * (g * _sigmoid(g))).astype(o_ref.dtype)


ATTN_ONES_ROWS = 16
ATTN_MAX_SUBTILES = 11


def _diff_attn(qt, k, vt, g, subln_g, lam, post_scale):
    b, _, nq = qt.shape
    nk = k.shape[1]
    tq = _pick(nq, (1024, 512, 256, 128))
    tsub = _pick(nk, (768, 512, 384, 256, 128))
    n_sub = nk // tsub
    tk = tsub * max(s for s in range(1, ATTN_MAX_SUBTILES + 1) if n_sub % s == 0)
    return pl.pallas_call(
        functools.partial(_diff_attn_kernel, tsub),
        grid=(b, DIFF_HEADS, nq // tq, nk // tk),
        in_specs=[pl.BlockSpec(memory_space=pltpu.SMEM),
                  pl.BlockSpec((1, LANES, tq), lambda bi, h, i, j: (bi, h, i)),
                  pl.BlockSpec((1, tk, LANES), lambda bi, h, i, j: (bi, j, h)),
                  pl.BlockSpec((1, LANES, tk), lambda bi, h, i, j: (bi, h, j)),
                  pl.BlockSpec((1, tq, LANES), lambda bi, h, i, j: (bi, i, h)),
                  pl.BlockSpec((DIFF_VD, 1), lambda bi, h, i, j: (0, 0))],
        out_specs=pl.BlockSpec((1, tq, LANES), lambda bi, h, i, j: (bi, i, h)),
        out_shape=jax.ShapeDtypeStruct((b, nq, DIFF_W), BF16),
        scratch_shapes=[pltpu.VMEM((LANES, 2 * tq), BF16), pltpu.VMEM((8, 2 * tq), F32),
                        pltpu.VMEM((8, 2 * tq), F32), pltpu.VMEM((DIFF_VD, 2 * tq), F32)],
        compiler_params=pltpu.CompilerParams(dimension_semantics=("arbitrary",) * 4, vmem_limit_bytes=VMEM_LIMIT),
        name="diff_attn",
    )(jnp.stack([lam, jnp.asarray(post_scale, F32)]).reshape(1, 2).astype(F32), qt, k, vt, g,
      subln_g.reshape(DIFF_VD, 1))


def _diff_prep_kernel(scale, q_ref, k_ref, v_ref, c_ref, s_ref, qt_ref, ko_ref, vt_ref):
    cos, sin = c_ref[...], s_ref[...]
    lane = lax.broadcasted_iota(jnp.int32, cos.shape, 1) % DIFF_HD
    half = DIFF_HD // 2

    def rope(x):
        partner = jnp.where(lane < half, pltpu.roll(x, LANES - half, 1), pltpu.roll(x, half, 1))
        return x * cos + partner * sin

    qt_ref[0] = (rope(q_ref[0]) * scale).T.astype(qt_ref.dtype)
    ko_ref[0] = rope(k_ref[0]).astype(ko_ref.dtype)
    vt_ref[0] = v_ref[0].T.astype(vt_ref.dtype)


def _diff_prep(ub, cos_t, sin_t, scale):
    b, n_all, _ = ub.shape
    tr = _pick(n_all, (768, 512, 384, 256, 128))
    col = lambda off: pl.BlockSpec((1, tr, LANES), lambda bi, i, h: (bi, i, off + h))
    tab = pl.BlockSpec((tr, LANES), lambda bi, i, h: (i, 0))
    feat = pl.BlockSpec((1, LANES, tr), lambda bi, i, h: (bi, h, i))
    return pl.pallas_call(
        functools.partial(_diff_prep_kernel, scale),
        grid=(b, n_all // tr, DIFF_HEADS),
        in_specs=[col(0), col(DIFF_HEADS), col(2 * DIFF_HEADS), tab, tab],
        out_specs=[feat, col(0), feat],
        out_shape=[jax.ShapeDtypeStruct((b, DIFF_W, n_all), BF16), jax.ShapeDtypeStruct((b, n_all, DIFF_W), BF16),
                   jax.ShapeDtypeStruct((b, DIFF_W, n_all), BF16)],
        compiler_params=pltpu.CompilerParams(dimension_semantics=("arbitrary",) * 3, vmem_limit_bytes=VMEM_LIMIT),
        name="diff_prep",
    )(ub, ub, ub, cos_t, sin_t)


SSD_GROUPS_PER_STEP = 8


def _ssd_scan_kernel(x_ref, b_ref, c_ref, dt_ref, bias_ref, alog_ref, y_ref, st_ref):
    t = SSD_CHUNK
    d = pl.program_id(1)
    gstep = pl.program_id(2)
    ngrp = x_ref.shape[2] // SSD_GW

    @pl.when(pl.program_id(3) == 0)
    def _():
        st_ref[...] = jnp.zeros_like(st_ref)

    dt = _softplus(dt_ref[0] + bias_ref[...])
    dta = dt * (-jnp.exp(alog_ref[...]))
    tr = lax.broadcasted_iota(jnp.int32, (t, t), 0)
    tc = lax.broadcasted_iota(jnp.int32, (t, t), 1)
    fwd = d == 0
    incl = jnp.where(fwd, tc - tr, tr - tc) <= 0
    x_all, b_all, c_all, st_all = x_ref[0], b_ref[0], c_ref[0], st_ref[...]
    outs = {}
    chains = []
    for j in range(ngrp):
        xs = slice(j * SSD_GW, (j + 1) * SSD_GW)
        ns = slice(j * SSD_STATE, (j + 1) * SSD_STATE)
        first_head = d * SSD_HEADS + (gstep * ngrp + j) * (SSD_HEADS // SSD_GROUPS)
        chains.append(_ssd_chain(x_all[:, xs], b_all[:, ns], c_all[:, ns], st_all[:, xs], dt, dta, first_head, fwd,
                                 incl, outs, j))
    while chains:
        alive = []
        for ch in chains:
            if next(ch, _DONE) is not _DONE:
                alive.append(ch)
        chains = alive
    for j in range(ngrp):
        xs = slice(j * SSD_GW, (j + 1) * SSD_GW)
        y, st_new = outs[j]
        y_ref[0, 0, :, xs] = y
        st_ref[:, xs] = st_new


def _ssd_chain(x, bm, cm, st, dt, dta, first_head, fwd, incl, outs, key):
    t = SSD_CHUNK
    e_heads = SSD_HEADS // SSD_GROUPS
    lane = lax.broadcasted_iota(jnp.int32, (t, 2 * SSD_HEADS), 1)
    lane_x = lax.broadcasted_iota(jnp.int32, (t, SSD_GW), 1) // SSD_HD
    lane_e = lax.broadcasted_iota(jnp.int32, (t, LANES), 1)
    dt_x = jnp.zeros((t, SSD_GW), F32)
    dta4 = jnp.zeros((t, LANES), F32)
    for e in range(e_heads):
        pick = lane == first_head + e
        dt_e = jnp.sum(jnp.where(pick, dt, 0.0), axis=1, keepdims=True)
        dta_e = jnp.sum(jnp.where(pick, dta, 0.0), axis=1, keepdims=True)
        dt_x = jnp.where(lane_x == e, dt_e, dt_x)
        dta4 = jnp.where(lane_e == e, dta_e, dta4)

    cum4 = _mm_sel(incl.astype(BF16), dta4)
    cb = _mm_nt(cm, bm)
    y_off_raw = _mm(cm, st)
    yield
    cum_t = cum4.T
    last4 = jnp.where(fwd, cum4[t - 1:t, :], cum4[0:1, :])
    cum_x = jnp.zeros((t, SSD_GW), F32)
    last_x = jnp.zeros((1, SSD_GW), F32)
    for e in range(e_heads):
        cum_x = jnp.where(lane_x == e, cum4[:, e:e + 1], cum_x)
        last_x = jnp.where(lane_x[:1] == e, last4[:, e:e + 1], last_x)

    xdt = x * dt_x
    zero_h = jnp.zeros((t, LANES), F32)
    lane_h = lax.broadcasted_iota(jnp.int32, (t, LANES), 1) < SSD_HD
    ydiag = []
    for pair in range(e_heads // 2):
        lmats = []
        for e in (2 * pair, 2 * pair + 1):
            seg = cum4[:, e:e + 1] - cum_t[e:e + 1, :]
            lmats.append(cb * jnp.exp(jnp.where(incl, seg, NEG_BIG)))
        xp = xdt[:, pair * LANES:(pair + 1) * LANES]
        xbd = jnp.concatenate([jnp.where(lane_h, xp, zero_h), jnp.where(lane_h, zero_h, xp)], axis=0)
        ydiag.append(_mm(jnp.concatenate(lmats, axis=1), xbd))
    states = _mm_tn(bm, xdt * jnp.exp(last_x - cum_x))
    yield
    y = jnp.concatenate(ydiag, axis=1) + y_off_raw * jnp.exp(cum_x)
    outs[key] = (y, st * jnp.exp(last_x) + states)


def _ssd_scan(xbc, dt_raw, dt_bias, a_log, n_ctx):
    b, n_all, _ = xbc.shape
    t = SSD_CHUNK
    nc = n_all // t
    ncc = n_ctx // t
    cidx = lambda d, i: _chunk_order(d, i, ncc, nc)
    gg = SSD_GROUPS_PER_STEP
    nsteps_g = SSD_GROUPS // gg
    xw, nw = gg * SSD_GW, gg * SSD_STATE
    b0 = SSD_W // nw
    row = pl.BlockSpec((1, 2 * SSD_HEADS), lambda bi, d, g, i: (0, 0))
    return pl.pallas_call(
        _ssd_scan_kernel,
        grid=(b, 2, nsteps_g, nc),
        in_specs=[pl.BlockSpec((1, t, xw), lambda bi, d, g, i: (bi, cidx(d, i), g)),
                  pl.BlockSpec((1, t, nw), lambda bi, d, g, i: (bi, cidx(d, i), b0 + g)),
                  pl.BlockSpec((1, t, nw), lambda bi, d, g, i: (bi, cidx(d, i), b0 + nsteps_g + g)),
                  pl.BlockSpec((1, t, 2 * SSD_HEADS), lambda bi, d, g, i: (bi, cidx(d, i), 0)),
                  row, row],
        out_specs=pl.BlockSpec((1, 1, t, xw), lambda bi, d, g, i: (bi, d, cidx(d, i), g)),
        out_shape=jax.ShapeDtypeStruct((b, 2, n_all, SSD_W), F32),
        scratch_shapes=[pltpu.VMEM((SSD_STATE, xw), F32)],
        compiler_params=pltpu.CompilerParams(dimension_semantics=("arbitrary",) * 4, vmem_limit_bytes=VMEM_LIMIT),
        name="ssd_scan",
    )(xbc, xbc, xbc, dt_raw, dt_bias.reshape(1, -1), a_log.reshape(1, -1))


def _rms_norm(x, g, eps=EPS):
    return x * lax.rsqrt(jnp.mean(x * x, axis=-1, keepdims=True) + eps) * g


HALO = 8


def _dwconv_kernel(n_ctx, n_all, silu, cur_ref, prev_ref, next_ref, w_ref, b_ref, o_ref):
    tr = cur_ref.shape[1]
    taps = w_ref.shape[0]
    pad = taps // 2
    ext = jnp.concatenate([prev_ref[0], cur_ref[0], next_ref[0]], axis=0)
    t = pl.program_id(1) * tr + lax.broadcasted_iota(jnp.int32, (tr, 1), 0)
    acc = cur_ref[0] * w_ref[pad:pad + 1, :]
    for o in range(-pad, pad + 1):
        if o == 0:
            continue
        ts = t + o
        lo, hi = (t, ts) if o > 0 else (ts, t)
        crosses = (lo < n_ctx) & (hi >= n_ctx)
        valid = (ts >= 0) & (ts < n_all) & jnp.logical_not(crosses)
        src = ext[HALO + o:HALO + o + tr]
        acc = acc + jnp.where(valid, src, 0.0) * w_ref[o + pad:o + pad + 1, :]
    if silu:
        acc = acc + b_ref[...]
        acc = acc * _sigmoid(acc)
    o_ref[0] = acc


def _dwconv(u, col0, ncols, w, bias, n_ctx, silu):
    b, n_all, _ = u.shape
    tr = _pick(n_all, (1408, 768, 512, 384, 256, 128))
    wt = _pick(math.gcd(ncols, col0) if col0 else ncols, (512, 256, 128))
    c0 = col0 // wt
    hb = tr // HALO
    last = n_all // HALO - 1
    if bias is None:
        bias = jnp.zeros((ncols,), F32)
    return pl.pallas_call(
        functools.partial(_dwconv_kernel, n_ctx, n_all, silu),
        grid=(b, n_all // tr, ncols // wt),
        in_specs=[pl.BlockSpec((1, tr, wt), lambda bi, i, j: (bi, i, c0 + j)),
                  pl.BlockSpec((1, HALO, wt), lambda bi, i, j: (bi, jnp.maximum(i * hb - 1, 0), c0 + j)),
                  pl.BlockSpec((1, HALO, wt), lambda bi, i, j: (bi, jnp.minimum((i + 1) * hb, last), c0 + j)),
                  pl.BlockSpec((w.shape[0], wt), lambda bi, i, j: (0, j)),
                  pl.BlockSpec((1, wt), lambda bi, i, j: (0, j))],
        out_specs=pl.BlockSpec((1, tr, wt), lambda bi, i, j: (bi, i, j)),
        out_shape=jax.ShapeDtypeStruct((b, n_all, ncols), F32),
        compiler_params=pltpu.CompilerParams(dimension_semantics=("arbitrary",) * 3, vmem_limit_bytes=VMEM_LIMIT),
        name="dwconv",
    )(u, u, u, w, bias.reshape(1, ncols))


def _row_select(tbl_ref, i, tm, rows_per_batch, n_ctx):
    tiles_per_batch = rows_per_batch // tm
    bi = i // tiles_per_batch
    row = (i % tiles_per_batch) * tm + lax.broadcasted_iota(jnp.int32, (tm, 1), 0)
    nb = tbl_ref.shape[0] - 1
    return jnp.where(row < n_ctx, tbl_ref[nb:nb + 1, :], tbl_ref[pl.ds(bi, 1), :])


def _prenorm_kernel(n_ctx, rows_per_batch, x_ref, g_ref, shift_ref, scale_ref, o_ref):
    i = pl.program_id(0)
    tm = x_ref.shape[0]
    x = x_ref[...]
    y = x * lax.rsqrt(jnp.mean(x * x, axis=-1, keepdims=True) + EPS) * g_ref[...]
    scale = _row_select(scale_ref, i, tm, rows_per_batch, n_ctx)
    shift = _row_select(shift_ref, i, tm, rows_per_batch, n_ctx)
    o_ref[...] = (y * (1.0 + scale) + shift).astype(o_ref.dtype)


def _prenorm(x2d, g, shift_tbl, scale_tbl, n_ctx, rows_per_batch):
    m, n = x2d.shape
    tm = _pick(rows_per_batch, (384, 256, 128))
    tbl = pl.BlockSpec(shift_tbl.shape, lambda i: (0, 0))
    return pl.pallas_call(
        functools.partial(_prenorm_kernel, n_ctx, rows_per_batch),
        grid=(m // tm,),
        in_specs=[pl.BlockSpec((tm, n), lambda i: (i, 0)), pl.BlockSpec((1, n), lambda i: (0, 0)), tbl, tbl],
        out_specs=pl.BlockSpec((tm, n), lambda i: (i, 0)),
        out_shape=jax.ShapeDtypeStruct((m, n), BF16),
        compiler_params=pltpu.CompilerParams(dimension_semantics=("arbitrary",), vmem_limit_bytes=VMEM_LIMIT),
        name="prenorm",
    )(x2d, g.reshape(1, n), shift_tbl, scale_tbl)


def _rwkv_finish_kernel(yf_ref, yb_ref, bonus_ref, g_ref, w_ref, b_ref, o_ref):
    y = yf_ref[0] + yb_ref[0]
    hr = lax.broadcasted_iota(jnp.int32, (LANES, LANES), 0) // RWKV_HD
    hc = lax.broadcasted_iota(jnp.int32, (LANES, LANES), 1) // RWKV_HD
    head_ones = (hr == hc).astype(BF16)
    outs = []
    for p in range(y.shape[1] // LANES):
        yp = y[:, p * LANES:(p + 1) * LANES]
        mu = _mm_sel_right(yp, head_ones) * (1.0 / RWKV_HD)
        dev = yp - mu
        var = _mm_sel_right(dev * dev, head_ones) * (1.0 / RWKV_HD)
        outs.append(dev * lax.rsqrt(var + RWKV_GN_EPS))
    yn = jnp.concatenate(outs, axis=1) * w_ref[...] + b_ref[...]
    g = g_ref[0]
    o_ref[0] = ((yn + bonus_ref[0]) * (g * _sigmoid(g))).astype(o_ref.dtype)


def _rwkv_finish(yf, yb, bonus, ua, g_col0, lnx_w, lnx_b):
    b, n_all, _ = yf.shape
    tr = _pick(n_all, (768, 512, 384, 256, 128))
    wt = 256
    g0 = g_col0 // wt
    vec = pl.BlockSpec((1, wt), lambda bi, i, j: (0, j))
    blk = lambda c0: pl.BlockSpec((1, tr, wt), lambda bi, i, j: (bi, i, c0 + j))
    return pl.pallas_call(
        _rwkv_finish_kernel,
        grid=(b, n_all // tr, RWKV_W // wt),
        in_specs=[blk(0), blk(0), blk(0), blk(g0), vec, vec],
        out_specs=blk(0),
        out_shape=jax.ShapeDtypeStruct((b, n_all, RWKV_W), BF16),
        compiler_params=pltpu.CompilerParams(dimension_semantics=("arbitrary",) * 3, vmem_limit_bytes=VMEM_LIMIT),
        name="rwkv_finish",
    )(yf, yb, bonus, ua, lnx_w.reshape(1, -1), lnx_b.reshape(1, -1))


def _ssd_finish_kernel(y_ref, x_ref, z_ref, d_ref, g_ref, o_ref):
    z = z_ref[0]
    y = (y_ref[0, 0] + y_ref[0, 1] + d_ref[...] * x_ref[0]) * (z * _sigmoid(z))
    outs = []
    for grp in range(y.shape[1] // SSD_GW):
        yg = y[:, grp * SSD_GW:(grp + 1) * SSD_GW]
        outs.append(yg * lax.rsqrt(jnp.mean(yg * yg, axis=-1, keepdims=True) + EPS))
    o_ref[0] = (jnp.concatenate(outs, axis=1) * g_ref[...]).astype(o_ref.dtype)


def _ssd_finish(y2, xact, uc, d_vec, norm_g):
    b, _, n_all, _ = y2.shape
    tr = _pick(n_all, (384, 256, 128))
    wt = 1024
    vec = pl.BlockSpec((1, wt), lambda bi, i, j: (0, j))
    blk = pl.BlockSpec((1, tr, wt), lambda bi, i, j: (bi, i, j))
    return pl.pallas_call(
        _ssd_finish_kernel,
        grid=(b, n_all // tr, SSD_W // wt),
        in_specs=[pl.BlockSpec((1, 2, tr, wt), lambda bi, i, j: (bi, 0, i, j)), blk, blk, vec, vec],
        out_specs=blk,
        out_shape=jax.ShapeDtypeStruct((b, n_all, SSD_W), BF16),
        compiler_params=pltpu.CompilerParams(dimension_semantics=("arbitrary",) * 3, vmem_limit_bytes=VMEM_LIMIT),
        name="ssd_finish",
    )(y2, xact, uc, d_vec.reshape(1, -1), norm_g.reshape(1, -1))


def _axial_rope(n_tok):
    rows = n_tok // GRID_W
    row = jnp.repeat(jnp.arange(rows, dtype=F32), GRID_W)
    col = jnp.tile(jnp.arange(GRID_W, dtype=F32), rows)
    n_freq = DIFF_HD // 4
    inv = ROPE_BASE ** (-jnp.arange(n_freq, dtype=F32) / n_freq)
    ang = jnp.concatenate([row[:, None] * inv, col[:, None] * inv], axis=-1)
    return jnp.cos(ang), jnp.sin(ang)


def _rwkv_branch(ua, n_ctx, conv_w, w0, w2, a0, a2, k_k, k_a, r_k, lnx_w, lnx_b):
    uconv = _dwconv(ua, 0, RWKV_CONV_CH, conv_w, None, n_ctx, False)
    m, n, q, yl, bonus = _rwkv_local(uconv, w0, w2, a0, a2, k_k, k_a, r_k)
    yf, yb = _rwkv_scan(m, n, q, yl, n_ctx)
    return _rwkv_finish(yf, yb, bonus, ua, RWKV_CONV_CH, lnx_w, lnx_b)


def _diff_branch(ub, n_ctx, cos_t, sin_t, lam_p, subln_g, lam_init):
    lam = jnp.exp(jnp.sum(lam_p[0] * lam_p[1])) - jnp.exp(jnp.sum(lam_p[2] * lam_p[3])) + lam_init
    scale = DIFF_HD ** -0.5 * math.log2(math.e)
    qt, k_all, vt = _diff_prep(ub, cos_t, sin_t, scale)
    g = ub[..., 3 * DIFF_W:]
    post = 1.0 - lam_init
    oc = _diff_attn(qt[:, :, :n_ctx], k_all[:, :n_ctx], vt[:, :, :n_ctx], g[:, :n_ctx], subln_g, lam, post)
    ol = _diff_attn(qt[:, :, n_ctx:], k_all, vt, g[:, n_ctx:], subln_g, lam, post)
    return jnp.concatenate([oc, ol], axis=1)


def _rope_tables(n_ctx, n_lat):
    cos, sin = _axial_rope(n_lat)
    cos4 = jnp.tile(cos, (1, 4))
    sin4 = jnp.tile(jnp.concatenate([-sin, sin], axis=1), (1, 2))
    ident = lambda v: jnp.full((n_ctx, LANES), v, F32)
    return jnp.concatenate([ident(1.0), cos4], axis=0), jnp.concatenate([ident(0.0), sin4], axis=0)


def _ssd_branch(uc, dt_raw, n_ctx, conv_w, conv_b, a_log, dt_bias, d_skip, norm_g):
    xact = _dwconv(uc, SSD_W, SSD_XBC, conv_w, conv_b, n_ctx, True)
    y2 = _ssd_scan(xact, dt_raw, dt_bias, a_log, n_ctx)
    return _ssd_finish(y2, xact, uc, jnp.repeat(d_skip, SSD_HD), norm_g)


def kernel(x, c, ctx, c_ctx, w_mod, b_mod, g_pre, g_post, w_in, rwkv_conv, rwkv_w0, rwkv_w2, rwkv_a0, rwkv_a2,
           rwkv_k_k, rwkv_k_a, rwkv_r_k, rwkv_lnx_w, rwkv_lnx_b, diff_lambda, diff_subln, ssd_conv_w, ssd_conv_b,
           ssd_a_log, ssd_dt_bias, ssd_d, ssd_norm, w_branch_a, w_branch_b, w_branch_c, w_out):
    b, n_lat, dm = x.shape
    n_ctx = ctx.shape[1]
    n_all = n_ctx + n_lat
    depth = w_in.shape[0]
    cos_t, sin_t = _rope_tables(n_ctx, n_lat)
    cond = jax.nn.silu(jnp.concatenate([c, c_ctx[None, :]], axis=0))
    cond = jnp.pad(cond, ((0, 8 - (b + 1)), (0, 0)))

    o_rw, o_g, o_dt = 0, RWKV_CONV_CH, RWKV_CONV_CH + RWKV_W
    src_dt = o_dt + 4 * DIFF_W + SSD_W + SSD_XBC
    pad_a = -(o_dt + 2 * SSD_HEADS) % 512

    xa = jnp.concatenate([ctx, x], axis=1)
    for li in range(depth):
        lam_init = 0.8 - 0.6 * math.exp(-0.3 * li)
        mod = _matmul(cond, w_mod, tn_cands=(1024, 512, 256, 128), layer=li)[:b + 1] + b_mod[li]
        shift_l, scale_l, gate_l = jnp.split(mod[:b], 3, axis=-1)
        shift_c, scale_c, gate_c = jnp.split(mod[b], 3, axis=-1)
        tbl = lambda vl, vc: jnp.concatenate([vl, vc[None, :]], axis=0)
        hb = _prenorm(xa.reshape(b * n_all, dm), g_pre[li], tbl(shift_l, shift_c), tbl(scale_l, scale_c), n_ctx,
                      n_all)
        wl = w_in[li]
        proj = lambda w: _matmul(hb, w.astype(BF16), tm_cands=(1536, 768, 512, 384, 256, 128)).reshape(b, n_all, -1)
        ua = proj(jnp.concatenate([wl[:, :o_dt], wl[:, src_dt:src_dt + 2 * SSD_HEADS],
                                   jnp.zeros((dm, pad_a), wl.dtype)], axis=1))
        ub = proj(wl[:, o_dt:o_dt + 4 * DIFF_W])
        uc = proj(wl[:, o_dt + 4 * DIFF_W:src_dt])
        ug = proj(wl[:, src_dt + 2 * SSD_HEADS:])

        oa = _rwkv_branch(ua, n_ctx, rwkv_conv[li], rwkv_w0[li], rwkv_w2[li], rwkv_a0[li], rwkv_a2[li],
                          rwkv_k_k[li], rwkv_k_a[li], rwkv_r_k[li], rwkv_lnx_w[li], rwkv_lnx_b[li])
        ob = _diff_branch(ub, n_ctx, cos_t, sin_t, diff_lambda[li], diff_subln[li], lam_init)
        oc = _ssd_branch(uc, ua[..., o_dt:o_dt + 2 * SSD_HEADS], n_ctx, ssd_conv_w[li], ssd_conv_b[li],
                         ssd_a_log[li], ssd_dt_bias[li], ssd_d[li], ssd_norm[li])

        flat = lambda t: t.astype(BF16).reshape(b * n_all, -1)
        mrg = _merge(flat(oa), flat(ob), flat(oc), w_branch_a[li].astype(BF16), w_branch_b[li].astype(BF16),
                     w_branch_c[li].astype(BF16), ug.reshape(b * n_all, 3 * dm))
        xa = _out_norm(mrg, w_out[li].astype(BF16), xa.reshape(b * n_all, dm), tbl(gate_l, gate_c), g_post[li],
                       n_ctx, n_all).reshape(b, n_all, dm)
    return xa[:, n_ctx:]
```
